```python
import math
import jax, jax.numpy as jnp
from jax import lax
import numpy as np

D_MODEL = 1024
BATCH = 32
SEQ = 2048
DEPTH = 2

GRID_W = 64
HEAD_DIM = 64
N_Q_HEADS = 8
N_KV_HEADS = 2
GROUP = N_Q_HEADS // N_KV_HEADS
ATTN_WIDTH = N_Q_HEADS * HEAD_DIM
KV_WIDTH = N_KV_HEADS * HEAD_DIM
CONV_CH = D_MODEL - ATTN_WIDTH
MIX_WIDTH = ATTN_WIDTH + CONV_CH
IN_WIDTH = ATTN_WIDTH + 2 * KV_WIDTH + 2 * CONV_CH
Q_BLOCK = 128
ROPE_THETA = 10000.0
ROPE_HALF = HEAD_DIM // 2
ROPE_FREQS = ROPE_HALF // 2
CONV_WIDTH = 31
CONV_PAD = CONV_WIDTH // 2
N_EXPERTS = 32
TOP_K = 4
D_EXPERT = 1024
SWIGLU_LIMIT = 7.0
SWIGLU_ALPHA = 1.702
MOE_BLOCK = 512
EPS = 1e-6

kernel_name = "hymba_style_attn_conformer_moe_encoder"


def rmsnorm(x, g):
    xf = x.astype(jnp.float32)
    y = xf * lax.rsqrt(jnp.mean(xf * xf, axis=-1, keepdims=True) + EPS)
    return (y * g.astype(jnp.float32)).astype(x.dtype)


def layernorm(x, g, b):
    xf = x.astype(jnp.float32)
    mu = jnp.mean(xf, axis=-1, keepdims=True)
    var = jnp.mean(jnp.square(xf - mu), axis=-1, keepdims=True)
    y = (xf - mu) * lax.rsqrt(var + EPS)
    return (y * g.astype(jnp.float32) + b.astype(jnp.float32)).astype(x.dtype)


def axial_rope_tables(seq_len):
    rows = seq_len // GRID_W
    freqs = ROPE_THETA ** (-jnp.arange(ROPE_FREQS, dtype=jnp.float32) / ROPE_FREQS)
    row_ang = jnp.arange(rows, dtype=jnp.float32)[:, None] * freqs
    col_ang = jnp.arange(GRID_W, dtype=jnp.float32)[:, None] * freqs
    row_full = jnp.broadcast_to(row_ang[:, None, :], (rows, GRID_W, ROPE_FREQS)).reshape(seq_len, ROPE_FREQS)
    col_full = jnp.broadcast_to(col_ang[None, :, :], (rows, GRID_W, ROPE_FREQS)).reshape(seq_len, ROPE_FREQS)
    return jnp.cos(row_full), jnp.sin(row_full), jnp.cos(col_full), jnp.sin(col_full)


def _rotate(x, cos, sin):
    x1, x2 = x[..., :ROPE_FREQS], x[..., ROPE_FREQS:]
    return jnp.concatenate([x1 * cos - x2 * sin, x2 * cos + x1 * sin], axis=-1)


def apply_axial_rope(x, tables):
    cr, sr, cc, sc = [t.astype(x.dtype)[None, :, None, :] for t in tables]
    return jnp.concatenate([_rotate(x[..., :ROPE_HALF], cr, sr),
                            _rotate(x[..., ROPE_HALF:], cc, sc)], axis=-1)


def blocked_gqa(q, k, v):
    B, S, _, Dh = q.shape
    nblk = S // Q_BLOCK
    scale = Dh ** -0.5
    qb = q.reshape(B, nblk, Q_BLOCK, N_KV_HEADS, GROUP, Dh).transpose(1, 0, 2, 3, 4, 5)

    def one_block(qblk):
        s = jnp.einsum('bqkgd,bskd->bkgqs', qblk, k).astype(jnp.float32) * scale
        p = jax.nn.softmax(s, axis=-1).astype(v.dtype)
        return jnp.einsum('bkgqs,bskd->bqkgd', p, v)

    o = lax.map(one_block, qb)
    return o.transpose(1, 0, 2, 3, 4, 5).reshape(B, S, N_Q_HEADS * Dh)


def conformer_conv(a, gate, w_dw, b_dw, g_cn, b_cn):
    u = a * jax.nn.sigmoid(gate)
    u = lax.conv_general_dilated(u, w_dw[:, None, :].astype(u.dtype), window_strides=(1,),
                                 padding=[(CONV_PAD, CONV_PAD)],
                                 dimension_numbers=('NWC', 'WIO', 'NWC'),
                                 feature_group_count=CONV_CH) + b_dw
    return jax.nn.silu(layernorm(u, g_cn, b_cn))


def moe_ffn(h, w_router, b_router, w1, b1, w2, b2):
    B, S, D = h.shape
    N = B * S
    NK = N * TOP_K
    hf = h.reshape(N, D)
    logits = (hf @ w_router).astype(jnp.float32) + b_router.astype(jnp.float32)
    top_vals, top_idx = lax.top_k(logits, TOP_K)
    gates = jax.nn.softmax(top_vals, axis=-1)
    flat_e = top_idx.reshape(NK)
    flat_t = jnp.repeat(jnp.arange(N, dtype=jnp.int32), TOP_K)
    flat_g = gates.reshape(NK)
    order = jnp.argsort(flat_e)
    se, st, sg = flat_e[order], flat_t[order], flat_g[order]
    counts = jnp.bincount(flat_e, length=N_EXPERTS)
    starts = jnp.cumsum(counts) - counts
    padded = (counts + MOE_BLOCK - 1) // MOE_BLOCK * MOE_BLOCK
    pstarts = jnp.cumsum(padded) - padded
    pends = pstarts + padded
    dest = pstarts[se] + (jnp.arange(NK, dtype=jnp.int32) - starts[se])
    P = (NK + MOE_BLOCK - 1) // MOE_BLOCK * MOE_BLOCK + N_EXPERTS * MOE_BLOCK
    n_blocks = P // MOE_BLOCK
    pad_t = jnp.zeros((P,), jnp.int32).at[dest].set(st)
    pad_g = jnp.zeros((P,), jnp.float32).at[dest].set(sg)
    block_start = jnp.arange(n_blocks, dtype=jnp.int32) * MOE_BLOCK
    block_e = jnp.minimum(jnp.searchsorted(pends, block_start, side='right'), N_EXPERTS - 1)

    def step(acc, blk):
        tok, g, e = blk
        xb = hf[tok]
        hu = xb @ w1[e] + b1[e]
        x_glu = jnp.minimum(hu[:, :D_EXPERT], SWIGLU_LIMIT)
        x_lin = jnp.clip(hu[:, D_EXPERT:], -SWIGLU_LIMIT, SWIGLU_LIMIT)
        act = (x_lin + 1.0) * (x_glu * jax.nn.sigmoid(SWIGLU_ALPHA * x_glu))
        y = act @ w2[e] + b2[e]
        return acc.at[tok].add(y * g[:, None].astype(y.dtype)), None

    out, _ = lax.scan(step, jnp.zeros_like(hf),
                      (pad_t.reshape(n_blocks, MOE_BLOCK), pad_g.reshape(n_blocks, MOE_BLOCK), block_e))
    return out.reshape(B, S, D)


def setup_inputs(seed: int = 0) -> dict:
    key = jax.random.key(seed)
    ks = jax.random.split(key, 26)
    L, D = DEPTH, D_MODEL

    def nrm(k, shape, scale):
        return jax.random.normal(k, shape, jnp.float32) * scale

    return {
        "x": nrm(ks[0], (BATCH, SEQ, D), 1.0),
        "c": nrm(ks[1], (BATCH, D), 1.0),
        "w_mod": nrm(ks[2], (L, D, 6 * D), 0.5 * D ** -0.5),
        "b_mod": nrm(ks[3], (L, 6 * D), 0.01),
        "g_mix": 1.0 + nrm(ks[4], (L, D), 0.02),
        "w_in": nrm(ks[5], (L, D, IN_WIDTH), D ** -0.5),
        "g_q": 1.0 + nrm(ks[6], (L, HEAD_DIM), 0.02),
        "g_k": 1.0 + nrm(ks[7], (L, HEAD_DIM), 0.02),
        "w_dw": nrm(ks[8], (L, CONV_WIDTH, CONV_CH), CONV_WIDTH ** -0.5),
        "b_dw": nrm(ks[9], (L, CONV_CH), 0.01),
        "g_cn": 1.0 + nrm(ks[10], (L, CONV_CH), 0.02),
        "b_cn": nrm(ks[11], (L, CONV_CH), 0.01),
        "w_out": nrm(ks[12], (L, MIX_WIDTH, D), MIX_WIDTH ** -0.5),
        "g_ffn": 1.0 + nrm(ks[13], (L, D), 0.02),
        "w_router": nrm(ks[14], (L, D, N_EXPERTS), D ** -0.5),
        "b_router": nrm(ks[15], (L, N_EXPERTS), 0.01),
        "w1": nrm(ks[16], (L, N_EXPERTS, D, 2 * D_EXPERT), D ** -0.5),
        "b1": nrm(ks[17], (L, N_EXPERTS, 2 * D_EXPERT), 0.01),
        "w2": nrm(ks[18], (L, N_EXPERTS, D_EXPERT, D), D_EXPERT ** -0.5),
        "b2": nrm(ks[19], (L, N_EXPERTS, D), 0.01),
        "g_final": 1.0 + nrm(ks[20], (D,), 0.02),
    }


def reference(x, c, w_mod, b_mod, g_mix, w_in, g_q, g_k, w_dw, b_dw, g_cn, b_cn,
              w_out, g_ffn, w_router, b_router, w1, b1, w2, b2, g_final):
    B, S, D = x.shape
    rope = axial_rope_tables(S)
    c_act = jax.nn.silu(c)
    split_at = [ATTN_WIDTH, ATTN_WIDTH + KV_WIDTH, ATTN_WIDTH + 2 * KV_WIDTH,
                ATTN_WIDTH + 2 * KV_WIDTH + CONV_CH]
    for l in range(DEPTH):
        mod = c_act @ w_mod[l] + b_mod[l]
        sh1, sc1, gt1, sh2, sc2, gt2 = [m[:, None, :] for m in jnp.split(mod, 6, axis=-1)]

        h = rmsnorm(x, g_mix[l]) * (1.0 + sc1) + sh1
        proj = h @ w_in[l]
        q, k, v, ca, cg = jnp.split(proj, split_at, axis=-1)
        q = apply_axial_rope(rmsnorm(q.reshape(B, S, N_Q_HEADS, HEAD_DIM), g_q[l]), rope)
        k = apply_axial_rope(rmsnorm(k.reshape(B, S, N_KV_HEADS, HEAD_DIM), g_k[l]), rope)
        v = v.reshape(B, S, N_KV_HEADS, HEAD_DIM)
        attn = blocked_gqa(q, k, v)
        conv = conformer_conv(ca, cg, w_dw[l], b_dw[l], g_cn[l], b_cn[l])
        mix = jnp.concatenate([attn, conv], axis=-1) @ w_out[l]
        x = x + gt1 * mix

        h = rmsnorm(x, g_ffn[l]) * (1.0 + sc2) + sh2
        x = x + gt2 * moe_ffn(h, w_router[l], b_router[l], w1[l], b1[l], w2[l], b2[l])
    return rmsnorm(x, g_final)
```

```python
import functools

import jax
import jax.numpy as jnp
from jax import lax
from jax.experimental import pallas as pl
from jax.experimental.pallas import tpu as pltpu

F32 = jnp.float32
BF16 = jnp.bfloat16
I32 = jnp.int32

HEAD_DIM = 64
N_Q_HEADS = 8
N_KV_HEADS = 2
ATTN_WIDTH = N_Q_HEADS * HEAD_DIM
KV_WIDTH = N_KV_HEADS * HEAD_DIM
GRID_W = 64
ROPE_THETA = 10000.0
ROPE_FREQS = HEAD_DIM // 4
CONV_WIDTH = 31
CONV_PAD = CONV_WIDTH // 2
TOP_K = 4
SWIGLU_LIMIT = 7.0
SWIGLU_ALPHA = 1.702
EPS = 1e-6

LANES = 128
VMEM_LIMIT = 56 * 1024 * 1024


def _cparams(*sem):
    return pltpu.CompilerParams(dimension_semantics=sem, vmem_limit_bytes=VMEM_LIMIT)


def _mod_kernel(c_ref, w_ref, b_ref, o_ref):
    c = c_ref[...]
    c_act = (c * jax.nn.sigmoid(c)).astype(BF16)
    o_ref[0] = jnp.dot(c_act, w_ref[0].astype(BF16), preferred_element_type=F32) + b_ref[0]


def _modulation(c, w_mod, b_mod):
    L, D, W = w_mod.shape
    B = c.shape[0]
    tn = 1536
    return pl.pallas_call(
        _mod_kernel,
        grid=(L, W // tn),
        in_specs=[pl.BlockSpec((B, D), lambda l, j: (0, 0)),
                  pl.BlockSpec((1, D, tn), lambda l, j: (l, 0, j)),
                  pl.BlockSpec((1, 1, tn), lambda l, j: (l, 0, j))],
        out_specs=pl.BlockSpec((1, B, tn), lambda l, j: (l, 0, j)),
        out_shape=jax.ShapeDtypeStruct((L, B, W), F32),
        compiler_params=_cparams("parallel", "parallel"),
        name="modulation",
    )(c, w_mod, b_mod.reshape(L, 1, W))


def _head_rmsnorm(t, gain, bd):
    msq = jnp.dot((t * t).astype(BF16), bd, preferred_element_type=F32)
    return t * lax.rsqrt(msq + EPS) * gain


def _rope(t, cos, sin):
    w = t.shape[1]
    reps = w // LANES
    cosw = jnp.concatenate([cos] * reps, axis=1) if reps > 1 else cos
    sinw = jnp.concatenate([sin] * reps, axis=1) if reps > 1 else sin
    lane = lax.broadcasted_iota(I32, t.shape, 1)
    first = (lane & ROPE_FREQS) == 0
    partner = jnp.where(first, pltpu.roll(t, w - ROPE_FREQS, 1), pltpu.roll(t, ROPE_FREQS, 1))
    return t * cosw + partner * sinw


def _inproj_kernel(x_ref, mod_ref, gmix_ref, w_ref, gq_ref, gk_ref, bd_ref, cos_ref, sin_ref,
                   q_ref, k_ref, v_ref, u_ref):
    x = x_ref[0]
    mod = mod_ref[0]
    sh1, sc1 = mod[0:1], mod[1:2]
    ms = jnp.mean(x * x, axis=-1, keepdims=True)
    xn = x * lax.rsqrt(ms + EPS) * gmix_ref[...]
    h = (xn * (1.0 + sc1) + sh1).astype(BF16)
    proj = jnp.dot(h, w_ref[...], preferred_element_type=F32)
    a0, a1, a2 = ATTN_WIDTH, ATTN_WIDTH + KV_WIDTH, ATTN_WIDTH + 2 * KV_WIDTH
    cw = (proj.shape[1] - a2) // 2
    q, k, v = proj[:, :a0], proj[:, a0:a1], proj[:, a1:a2]
    ca, cg = proj[:, a2:a2 + cw], proj[:, a2 + cw:]
    cos, sin = cos_ref[...], sin_ref[...]
    bd = bd_ref[...]
    q = _rope(_head_rmsnorm(q, gq_ref[...], bd), cos, sin)
    k = _rope(_head_rmsnorm(k, gk_ref[...], bd[:KV_WIDTH, :KV_WIDTH]), cos, sin)
    q_ref[0] = (q * (HEAD_DIM ** -0.5)).astype(BF16)
    k_ref[0] = k.astype(BF16)
    v_ref[0] = v.astype(BF16)
    u_ref[0] = ca * jax.nn.sigmoid(cg)


def _inproj(x, mod_l, g_mix, w_in, g_q, g_k, bd, cos_t, sin_t, ts):
    B, S, D = x.shape
    W = w_in.shape[1]
    cw = (W - ATTN_WIDTH - 2 * KV_WIDTH) // 2
    gq = jnp.tile(g_q, N_Q_HEADS).reshape(1, ATTN_WIDTH)
    gk = jnp.tile(g_k, N_KV_HEADS).reshape(1, KV_WIDTH)
    const = lambda b, s: (0, 0)
    return pl.pallas_call(
        _inproj_kernel,
        grid=(B, S // ts),
        in_specs=[pl.BlockSpec((1, ts, D), lambda b, s: (b, s, 0)),
                  pl.BlockSpec((1, 6, D), lambda b, s: (b, 0, 0)),
                  pl.BlockSpec((1, D), const),
                  pl.BlockSpec((D, W), const),
                  pl.BlockSpec((1, ATTN_WIDTH), const),
                  pl.BlockSpec((1, KV_WIDTH), const),
                  pl.BlockSpec((ATTN_WIDTH, ATTN_WIDTH), const),
                  pl.BlockSpec((ts, LANES), lambda b, s: (s, 0)),
                  pl.BlockSpec((ts, LANES), lambda b, s: (s, 0))],
        out_specs=[pl.BlockSpec((1, ts, ATTN_WIDTH), lambda b, s: (b, s, 0)),
                   pl.BlockSpec((1, ts, KV_WIDTH), lambda b, s: (b, s, 0)),
                   pl.BlockSpec((1, ts, KV_WIDTH), lambda b, s: (b, s, 0)),
                   pl.BlockSpec((1, ts, cw), lambda b, s: (b, s, 0))],
        out_shape=[jax.ShapeDtypeStruct((B, S, ATTN_WIDTH), BF16),
                   jax.ShapeDtypeStruct((B, S, KV_WIDTH), BF16),
                   jax.ShapeDtypeStruct((B, S, KV_WIDTH), BF16),
                   jax.ShapeDtypeStruct((B, S, cw), F32)],
        compiler_params=_cparams("parallel", "parallel"),
        name="inproj",
    )(x, mod_l, g_mix.reshape(1, D), w_in, gq, gk, bd, cos_t, sin_t)


def _attn_kernel(q_ref, k_ref, v_ref, o_ref):
    k = k_ref[0]
    v = v_ref[0]
    tq = q_ref.shape[1]
    lane = lax.broadcasted_iota(I32, (tq, LANES), 1)
    low = lane < HEAD_DIM
    heads_per_tile = LANES // HEAD_DIM
    group = N_Q_HEADS // N_KV_HEADS
    for j in range(ATTN_WIDTH // LANES):
        qt = q_ref[0, :, j * LANES:(j + 1) * LANES].astype(F32)
        g = (j * heads_per_tile) // group
        placed = []
        for hh in range(heads_per_tile):
            qm = jnp.where(low if hh == 0 else jnp.logical_not(low), qt, 0.0)
            if hh != g:
                qm = pltpu.roll(qm, HEAD_DIM, 1)
            s = lax.dot_general(qm.astype(BF16), k, (((1,), (1,)), ((), ())),
                                preferred_element_type=F32)
            m = jnp.max(s, axis=-1, keepdims=True)
            p = jnp.exp(s - m)
            l = jnp.sum(p, axis=-1, keepdims=True)
            o = jnp.dot(p.astype(BF16), v, preferred_element_type=F32) / l
            if hh != g:
                o = pltpu.roll(o, HEAD_DIM, 1)
            placed.append(o)
        o_ref[0, :, j * LANES:(j + 1) * LANES] = jnp.where(low, placed[0], placed[1]).astype(BF16)


def _attention(q, k, v, tq):
    B, S, _ = q.shape
    return pl.pallas_call(
        _attn_kernel,
        grid=(B, S // tq),
        in_specs=[pl.BlockSpec((1, tq, ATTN_WIDTH), lambda b, i: (b, i, 0)),
                  pl.BlockSpec((1, S, KV_WIDTH), lambda b, i: (b, 0, 0)),
                  pl.BlockSpec((1, S, KV_WIDTH), lambda b, i: (b, 0, 0))],
        out_specs=pl.BlockSpec((1, tq, ATTN_WIDTH), lambda b, i: (b, i, 0)),
        out_shape=jax.ShapeDtypeStruct((B, S, ATTN_WIDTH), BF16),
        compiler_params=_cparams("parallel", "parallel"),
        name="attention",
    )(q, k, v)


def _conv_kernel(u_ref, w_ref, b_ref, g_ref, beta_ref, o_ref, pad_ref, *, tc):
    S, C = u_ref.shape[1], u_ref.shape[2]
    halo = 16
    pad_ref[0:halo, :] = jnp.zeros((halo, C), F32)
    pad_ref[halo + S:halo + S + halo, :] = jnp.zeros((halo, C), F32)
    pad_ref[halo:halo + S, :] = u_ref[0]

    sub = 8
    span = tc + (CONV_WIDTH + halo - CONV_PAD + sub - 1) // sub * sub

    def tile(i, carry):
        s0 = pl.multiple_of(i * tc, tc)
        win = pad_ref[pl.ds(s0, span), :]
        acc = jnp.zeros((tc, C), F32)
        for r in range(sub):
            shifted = win if r == 0 else win[r:r + span - sub, :]
            for j in range(CONV_WIDTH):
                off = halo - CONV_PAD + j
                if off % sub == r:
                    a = off - r
                    acc = acc + shifted[a:a + tc, :] * w_ref[j:j + 1, :]
        acc = acc + b_ref[...]
        mu = jnp.mean(acc, axis=-1, keepdims=True)
        d = acc - mu
        var = jnp.mean(d * d, axis=-1, keepdims=True)
        y = d * lax.rsqrt(var + EPS) * g_ref[...] + beta_ref[...]
        o_ref[0, pl.ds(s0, tc), :] = (y * jax.nn.sigmoid(y)).astype(BF16)
        return carry

    lax.fori_loop(0, S // tc, tile, 0)


def _conv(u, w_dw, b_dw, g_cn, b_cn, tc=64):
    B, S, C = u.shape
    const = lambda b: (0, 0)
    return pl.pallas_call(
        functools.partial(_conv_kernel, tc=tc),
        grid=(B,),
        in_specs=[pl.BlockSpec((1, S, C), lambda b: (b, 0, 0)),
                  pl.BlockSpec((CONV_WIDTH, C), const),
                  pl.BlockSpec((1, C), const),
                  pl.BlockSpec((1, C), const),
                  pl.BlockSpec((1, C), const)],
        out_specs=pl.BlockSpec((1, S, C), lambda b: (b, 0, 0)),
        out_shape=jax.ShapeDtypeStruct((B, S, C), BF16),
        scratch_shapes=[pltpu.VMEM((S + 32, C), F32)],
        compiler_params=_cparams("parallel"),
        name="conv",
    )(u, w_dw, b_dw.reshape(1, C), g_cn.reshape(1, C), b_cn.reshape(1, C))


def _split_bf16(a):
    hi = a.astype(BF16)
    lo = (a - hi.astype(F32)).astype(BF16)
    return hi, lo


def _outproj_router_kernel(x_ref, attn_ref, conv_ref, wout_ref, mod_ref, gffn_ref, wr_ref, br_ref, tri_ref,
                           x1_ref, h2_ref, idx_ref, gate_ref, rank_ref, cnt_ref, carry_ref):
    first_step = jnp.logical_and(pl.program_id(0) == 0, pl.program_id(1) == 0)

    @pl.when(first_step)
    def _():
        carry_ref[...] = jnp.zeros_like(carry_ref)

    mod = mod_ref[0]
    gt1, sh2, sc2 = mod[2:3], mod[3:4], mod[4:5]
    aw = attn_ref.shape[2]
    mix = (jnp.dot(attn_ref[0], wout_ref[:aw, :], preferred_element_type=F32)
           + jnp.dot(conv_ref[0], wout_ref[aw:, :], preferred_element_type=F32))
    x1 = x_ref[0] + gt1 * mix
    x1_ref[0] = x1
    ms = jnp.mean(x1 * x1, axis=-1, keepdims=True)
    h2 = x1 * lax.rsqrt(ms + EPS) * gffn_ref[...] * (1.0 + sc2) + sh2
    h2_ref[...] = h2

    h_hi, h_lo = _split_bf16(h2)
    w_hi, w_lo = _split_bf16(wr_ref[...])
    nt = (((1,), (1,)), ((), ()))
    logits = (lax.dot_general(w_hi, h_hi, nt, preferred_element_type=F32)
              + lax.dot_general(w_lo, h_hi, nt, preferred_element_type=F32)
              + lax.dot_general(w_hi, h_lo, nt, preferred_element_type=F32)) + br_ref[...]
    n_exp, ts = logits.shape
    eio = lax.broadcasted_iota(I32, logits.shape, 0).astype(F32)
    vals, sels = [], []
    work = logits
    for _ in range(TOP_K):
        m = jnp.max(work, axis=0, keepdims=True)
        ik = jnp.min(jnp.where(work == m, eio, float(n_exp)), axis=0, keepdims=True)
        hit = eio == ik
        work = jnp.where(hit, -jnp.inf, work)
        vals.append(m)
        sels.append((ik, hit))
    ex = [jnp.exp(v - vals[0]) for v in vals]
    den = ex[0] + ex[1] + ex[2] + ex[3]

    chosen = jnp.zeros(logits.shape, F32)
    for _, hit in sels:
        chosen = chosen + hit.astype(F32)
    prefix = jnp.dot(chosen.astype(BF16), tri_ref[...], preferred_element_type=F32)
    base = jnp.concatenate([carry_ref[...]] * (ts // LANES), axis=1) + prefix
    for kk, (ik, hit) in enumerate(sels):
        idx_ref[kk:kk + 1, :] = ik.astype(I32)
        rank_ref[kk:kk + 1, :] = jnp.sum(jnp.where(hit, base, 0.0), axis=0, keepdims=True).astype(I32)
    carry_ref[...] = carry_ref[...] + jnp.sum(chosen, axis=1, keepdims=True)
    cnt_ref[...] = carry_ref[...]

    gpad = jnp.concatenate([e / den for e in ex] + [jnp.zeros((LANES - TOP_K, ts), F32)], axis=0)
    gate_ref[...] = gpad.T


def _outproj_router(x, attn, conv, w_out, mod_l, g_ffn, w_router, b_router, tri, ts):
    B, S, D = x.shape
    N = B * S
    E = w_router.shape[1]
    nts = S // ts
    aw, cw = attn.shape[2], conv.shape[2]
    const = lambda b, s: (0, 0)
    tok = lambda b, s: (0, b * nts + s)
    return pl.pallas_call(
        _outproj_router_kernel,
        grid=(B, nts),
        in_specs=[pl.BlockSpec((1, ts, D), lambda b, s: (b, s, 0)),
                  pl.BlockSpec((1, ts, aw), lambda b, s: (b, s, 0)),
                  pl.BlockSpec((1, ts, cw), lambda b, s: (b, s, 0)),
                  pl.BlockSpec((aw + cw, D), const),
                  pl.BlockSpec((1, 6, D), lambda b, s: (b, 0, 0)),
                  pl.BlockSpec((1, D), const),
                  pl.BlockSpec((E, D), const),
                  pl.BlockSpec((E, 1), const),
                  pl.BlockSpec((ts, ts), const)],
        out_specs=[pl.BlockSpec((1, ts, D), lambda b, s: (b, s, 0)),
                   pl.BlockSpec((ts, D), lambda b, s: (b * nts + s, 0)),
                   pl.BlockSpec((TOP_K, ts), tok),
                   pl.BlockSpec((ts, LANES), lambda b, s: (b * nts + s, 0)),
                   pl.BlockSpec((TOP_K, ts), tok),
                   pl.BlockSpec((E, LANES), const)],
        out_shape=[jax.ShapeDtypeStruct((B, S, D), F32),
                   jax.ShapeDtypeStruct((N, D), F32),
                   jax.ShapeDtypeStruct((TOP_K, N), I32),
                   jax.ShapeDtypeStruct((N, LANES), F32),
                   jax.ShapeDtypeStruct((TOP_K, N), I32),
                   jax.ShapeDtypeStruct((E, LANES), F32)],
        scratch_shapes=[pltpu.VMEM((E, LANES), F32)],
        compiler_params=_cparams("arbitrary", "arbitrary"),
        name="outproj_router",
    )(x, attn, conv, w_out, mod_l, g_ffn.reshape(1, D), w_router.T, b_router.reshape(E, 1), tri)


def _dest_kernel(pstart_ref, idx_ref, rank_ref, dest_ref):
    idx = idx_ref[...]
    acc = rank_ref[...]
    for e in range(pstart_ref.shape[0]):
        acc = acc + jnp.where(idx == e, pstart_ref[e], 0)
    dest_ref[...] = acc


def _dest_rows(pstart, idx, rank, tn):
    K, N = idx.shape
    return pl.pallas_call(
        _dest_kernel,
        grid_spec=pltpu.PrefetchScalarGridSpec(
            num_scalar_prefetch=1,
            grid=(N // tn,),
            in_specs=[pl.BlockSpec((K, tn), lambda i, p: (0, i)),
                      pl.BlockSpec((K, tn), lambda i, p: (0, i))],
            out_specs=pl.BlockSpec((K, tn), lambda i, p: (0, i))),
        out_shape=jax.ShapeDtypeStruct((K, N), I32),
        compiler_params=_cparams("parallel"),
        name="dest_rows",
    )(pstart, idx, rank)


def _dispatch_kernel(cnt_ref, pstart_ref, pend_ref, dest_ref, h_ref, xs_ref, zero_ref, sem, zsem):
    ts = h_ref.shape[0]

    def row_copy(t, kk):
        return pltpu.make_async_copy(h_ref.at[pl.ds(t, 1)], xs_ref.at[pl.ds(dest_ref[kk, t], 1)], sem)

    def issue(t, c):
        for kk in range(TOP_K):
            row_copy(t, kk).start()
        return c

    lax.fori_loop(0, ts, issue, 0)

    @pl.when(pl.program_id(0) == pl.num_programs(0) - 1)
    def _():
        zero_ref[...] = jnp.zeros_like(zero_ref)

        def zero_copy(r):
            return pltpu.make_async_copy(zero_ref.at[pl.ds(0, 1)], xs_ref.at[pl.ds(r, 1)], zsem)

        for e in range(cnt_ref.shape[0]):
            lo = pstart_ref[e] + cnt_ref[e]
            hi = pend_ref[e]
            lax.fori_loop(lo, hi, lambda r, c: (zero_copy(r).start(), c)[1], 0)
            lax.fori_loop(lo, hi, lambda r, c: (zero_copy(r).wait(), c)[1], 0)

    def drain(t, c):
        for kk in range(TOP_K):
            row_copy(t, kk).wait()
        return c

    lax.fori_loop(0, ts, drain, 0)


def _dispatch(cnt, pstart, pend, dest, h2, n_rows, ts):
    N, D = h2.shape
    return pl.pallas_call(
        _dispatch_kernel,
        grid_spec=pltpu.PrefetchScalarGridSpec(
            num_scalar_prefetch=3,
            grid=(N // ts,),
            in_specs=[pl.BlockSpec((TOP_K, ts), lambda i, *_: (0, i), memory_space=pltpu.SMEM),
                      pl.BlockSpec((ts, D), lambda i, *_: (i, 0))],
            out_specs=pl.BlockSpec(memory_space=pl.ANY),
            scratch_shapes=[pltpu.VMEM((8, D), F32),
                            pltpu.SemaphoreType.DMA,
                            pltpu.SemaphoreType.DMA]),
        out_shape=jax.ShapeDtypeStruct((n_rows, D), F32),
        compiler_params=_cparams("arbitrary"),
        name="dispatch",
    )(cnt, pstart, pend, dest, h2)


def _expert_kernel(be_ref, nu_ref, x_ref, w1_ref, b1_ref, w2_ref, b2_ref, y_ref):
    @pl.when(pl.program_id(0) < nu_ref[0])
    def _():
        dx = w2_ref.shape[1]
        hu = jnp.dot(x_ref[...].astype(BF16), w1_ref[0], preferred_element_type=F32) + b1_ref[0]
        x_glu = jnp.minimum(hu[:, :dx], SWIGLU_LIMIT)
        x_lin = jnp.clip(hu[:, dx:], -SWIGLU_LIMIT, SWIGLU_LIMIT)
        act = (x_lin + 1.0) * (x_glu * jax.nn.sigmoid(SWIGLU_ALPHA * x_glu))
        y_ref[...] = jnp.dot(act.astype(BF16), w2_ref[0], preferred_element_type=F32) + b2_ref[0]


def _experts(block_e, n_used, xs, w1, b1, w2, b2, tm):
    P, D = xs.shape
    E, _, H = w1.shape
    dx = w2.shape[1]
    row = lambda i, be, nu: (jnp.minimum(i, nu[0] - 1), 0)
    exp3 = lambda i, be, nu: (be[i], 0, 0)
    return pl.pallas_call(
        _expert_kernel,
        grid_spec=pltpu.PrefetchScalarGridSpec(
            num_scalar_prefetch=2,
            grid=(P // tm,),
            in_specs=[pl.BlockSpec((tm, D), row),
                      pl.BlockSpec((1, D, H), exp3),
                      pl.BlockSpec((1, 1, H), exp3),
                      pl.BlockSpec((1, dx, D), exp3),
                      pl.BlockSpec((1, 1, D), exp3)],
            out_specs=pl.BlockSpec((tm, D), row)),
        out_shape=jax.ShapeDtypeStruct((P, D), F32),
        compiler_params=_cparams("arbitrary"),
        name="experts",
    )(block_e, n_used, xs, w1, b1.reshape(E, 1, H), w2, b2.reshape(E, 1, D))


def _combine_kernel(dest_ref, x_ref, mod_ref, gate_ref, gfin_ref, ys_ref, o_ref, buf_ref, sem, *, final):
    ts = x_ref.shape[1]

    def row_copy(t, kk):
        return pltpu.make_async_copy(ys_ref.at[pl.ds(dest_ref[kk, t], 1)], buf_ref.at[kk, pl.ds(t, 1)], sem)

    def issue(t, c):
        for kk in range(TOP_K):
            row_copy(t, kk).start()
        return c

    def drain(t, c):
        for kk in range(TOP_K):
            row_copy(t, kk).wait()
        return c

    lax.fori_loop(0, ts, issue, 0)
    lax.fori_loop(0, ts, drain, 0)

    gates = gate_ref[...]
    moe = buf_ref[0] * gates[:, 0:1]
    for kk in range(1, TOP_K):
        moe = moe + buf_ref[kk] * gates[:, kk:kk + 1]
    gt2 = mod_ref[0][5:6]
    out = x_ref[0] + gt2 * moe
    if final:
        ms = jnp.mean(out * out, axis=-1, keepdims=True)
        out = out * lax.rsqrt(ms + EPS) * gfin_ref[...]
    o_ref[0] = out


def _combine(dest, x1, mod_l, gates_t, g_final, ys, ts, final):
    B, S, D = x1.shape
    nts = S // ts
    return pl.pallas_call(
        functools.partial(_combine_kernel, final=final),
        grid=(B, nts),
        in_specs=[pl.BlockSpec((TOP_K, ts), lambda b, s: (0, b * nts + s), memory_space=pltpu.SMEM),
                  pl.BlockSpec((1, ts, D), lambda b, s: (b, s, 0)),
                  pl.BlockSpec((1, 6, D), lambda b, s: (b, 0, 0)),
                  pl.BlockSpec((ts, LANES), lambda b, s: (b * nts + s, 0)),
                  pl.BlockSpec((1, D), lambda b, s: (0, 0)),
                  pl.BlockSpec(memory_space=pl.ANY)],
        out_specs=pl.BlockSpec((1, ts, D), lambda b, s: (b, s, 0)),
        out_shape=jax.ShapeDtypeStruct((B, S, D), F32),
        scratch_shapes=[pltpu.VMEM((TOP_K, ts, D), F32), pltpu.SemaphoreType.DMA],
        compiler_params=_cparams("arbitrary", "arbitrary"),
        name="combine",
    )(dest, x1, mod_l, gates_t, g_final.reshape(1, D), ys)


def _rope_tables(seq_len):
    freqs = ROPE_THETA ** (-jnp.arange(ROPE_FREQS, dtype=F32) / ROPE_FREQS)
    pos = jnp.arange(seq_len, dtype=I32)
    row_ang = (pos // GRID_W).astype(F32)[:, None] * freqs
    col_ang = (pos % GRID_W).astype(F32)[:, None] * freqs
    cos = jnp.concatenate([jnp.cos(row_ang)] * 2 + [jnp.cos(col_ang)] * 2, axis=1)
    sin = jnp.concatenate([-jnp.sin(row_ang), jnp.sin(row_ang), -jnp.sin(col_ang), jnp.sin(col_ang)], axis=1)
    reps = LANES // HEAD_DIM
    return jnp.tile(cos, (1, reps)), jnp.tile(sin, (1, reps))


def _tile_rows(n, prefs):
    for t in prefs:
        if n % t == 0:
            return t
    return n


def kernel(x, c, w_mod, b_mod, g_mix, w_in, g_q, g_k, w_dw, b_dw, g_cn, b_cn,
           w_out, g_ffn, w_router, b_router, w1, b1, w2, b2, g_final):
    B, S, D = x.shape
    L = w_mod.shape[0]
    E = w_router.shape[2]
    N = B * S
    ts = _tile_rows(S, (512, 256, 128))
    tq = _tile_rows(S, (256, 128))
    tm = 512
    n_rows = (N * TOP_K + tm - 1) // tm * tm + E * tm
    n_blocks = n_rows // tm

    mod = _modulation(c, w_mod, b_mod).reshape(L, B, 6, D)
    cos_t, sin_t = _rope_tables(S)
    head = jnp.arange(ATTN_WIDTH, dtype=I32) // HEAD_DIM
    bd = jnp.where(head[:, None] == head[None, :], 1.0 / HEAD_DIM, 0.0).astype(BF16)
    tok = jnp.arange(ts, dtype=I32)
    tri = (tok[:, None] < tok[None, :]).astype(BF16)

    for l in range(L):
        q, k, v, u = _inproj(x, mod[l], g_mix[l], w_in[l].astype(BF16), g_q[l], g_k[l], bd, cos_t, sin_t, ts)
        attn = _attention(q, k, v, tq)
        conv = _conv(u, w_dw[l], b_dw[l], g_cn[l], b_cn[l])
        x1, h2, idx, gates_t, rank, counts = _outproj_router(
            x, attn, conv, w_out[l].astype(BF16), mod[l], g_ffn[l], w_router[l], b_router[l], tri, ts)

        cnt = counts[:, 0].astype(I32)
        padded = (cnt + tm - 1) // tm * tm
        pend = jnp.cumsum(padded)
        pstart = pend - padded
        n_used = (pend[-1] // tm).astype(I32)
        blk = jnp.minimum(jnp.arange(n_blocks, dtype=I32), n_used - 1) * tm
        block_e = jnp.minimum(jnp.searchsorted(pend, blk, side='right'), E - 1).astype(I32)

        dest = _dest_rows(pstart, idx, rank, _tile_rows(N, (8192, 4096, 2048, 1024, 512)))
        xs = _dispatch(cnt, pstart, pend, dest, h2, n_rows, ts)
        ys = _experts(block_e, n_used.reshape(1), xs, w1[l].astype(BF16), b1[l], w2[l].astype(BF16), b2[l], tm)
        x = _combine(dest, x1, mod[l], gates_t, g_final, ys, ts, final=(l == L - 1))
    return x
```

```python
import functools

import jax
import jax.numpy as jnp
from jax import lax
from jax.experimental import pallas as pl
from jax.experimental.pallas import tpu as pltpu

F32 = jnp.float32
BF16 = jnp.bfloat16
I32 = jnp.int32

HEAD_DIM = 64
N_Q_HEADS = 8
N_KV_HEADS = 2
ATTN_WIDTH = N_Q_HEADS * HEAD_DIM
KV_WIDTH = N_KV_HEADS * HEAD_DIM
GRID_W = 64
ROPE_THETA = 10000.0
ROPE_FREQS = HEAD_DIM // 4
CONV_WIDTH = 31
CONV_PAD = CONV_WIDTH // 2
TOP_K = 4
SWIGLU_LIMIT = 7.0
SWIGLU_ALPHA = 1.702
EPS = 1e-6
LOG2E = 1.4426950408889634

LANES = 128
SUB = 8
VMEM_LIMIT = 56 * 1024 * 1024


def _cparams(*sem):
    return pltpu.CompilerParams(dimension_semantics=sem, vmem_limit_bytes=VMEM_LIMIT)


def _stage_rows(ts, n_exp):
    return ts * TOP_K + n_exp * SUB


def _mod_kernel(c_ref, w_ref, b_ref, o_ref):
    c = c_ref[...]
    c_act = (c * jax.nn.sigmoid(c)).astype(BF16)
    o_ref[0] = jnp.dot(c_act, w_ref[0].astype(BF16), preferred_element_type=F32) + b_ref[0]


def _modulation(c, w_mod, b_mod):
    L, D, W = w_mod.shape
    B = c.shape[0]
    tn = 1536
    return pl.pallas_call(
        _mod_kernel,
        grid=(L, W // tn),
        in_specs=[pl.BlockSpec((B, D), lambda l, j: (0, 0)),
                  pl.BlockSpec((1, D, tn), lambda l, j: (l, 0, j)),
                  pl.BlockSpec((1, 1, tn), lambda l, j: (l, 0, j))],
        out_specs=pl.BlockSpec((1, B, tn), lambda l, j: (l, 0, j)),
        out_shape=jax.ShapeDtypeStruct((L, B, W), F32),
        compiler_params=_cparams("parallel", "parallel"),
        name="modulation",
    )(c, w_mod, b_mod.reshape(L, 1, W))


def _head_rmsnorm(t, gain, bd):
    msq = jnp.dot((t * t).astype(BF16), bd, preferred_element_type=F32)
    return t * lax.rsqrt(msq + EPS) * gain


def _rope(t, cos, sin):
    w = t.shape[1]
    reps = w // LANES
    cosw = jnp.concatenate([cos] * reps, axis=1) if reps > 1 else cos
    sinw = jnp.concatenate([sin] * reps, axis=1) if reps > 1 else sin
    lane = lax.broadcasted_iota(I32, t.shape, 1)
    first = (lane & ROPE_FREQS) == 0
    partner = jnp.where(first, pltpu.roll(t, w - ROPE_FREQS, 1), pltpu.roll(t, ROPE_FREQS, 1))
    return t * cosw + partner * sinw


def _inproj_kernel(x_ref, mod_ref, gmix_ref, w_ref, gq_ref, gk_ref, bd_ref, cos_ref, sin_ref,
                   q_ref, k_ref, v_ref, u_ref, wb_ref):
    @pl.when(jnp.logical_and(pl.program_id(0) == 0, pl.program_id(1) == 0))
    def _():
        wb_ref[...] = w_ref[0].astype(BF16)

    x = x_ref[0]
    mod = mod_ref[0]
    sh1, sc1 = mod[0:1], mod[1:2]
    ms = jnp.mean(x * x, axis=-1, keepdims=True)
    xn = x * lax.rsqrt(ms + EPS) * gmix_ref[...]
    h = (xn * (1.0 + sc1) + sh1).astype(BF16)
    proj = jnp.dot(h, wb_ref[...], preferred_element_type=F32)
    a0, a1, a2 = ATTN_WIDTH, ATTN_WIDTH + KV_WIDTH, ATTN_WIDTH + 2 * KV_WIDTH
    cw = (proj.shape[1] - a2) // 2
    q, k, v = proj[:, :a0], proj[:, a0:a1], proj[:, a1:a2]
    ca, cg = proj[:, a2:a2 + cw], proj[:, a2 + cw:]
    cos, sin = cos_ref[...], sin_ref[...]
    bd = bd_ref[...]
    q = _rope(_head_rmsnorm(q, gq_ref[...], bd), cos, sin)
    k = _rope(_head_rmsnorm(k, gk_ref[...], bd[:KV_WIDTH, :KV_WIDTH]), cos, sin)
    q_ref[0] = (q * (HEAD_DIM ** -0.5 * LOG2E)).astype(BF16)
    k_ref[0] = k.astype(BF16)
    v_ref[0] = v.astype(BF16)
    u_ref[0] = ca * jax.nn.sigmoid(cg)


def _inproj(x, mod, layer, g_mix, w_in, g_q, g_k, bd, cos_t, sin_t, ts):
    B, S, D = x.shape
    W = w_in.shape[2]
    cw = (W - ATTN_WIDTH - 2 * KV_WIDTH) // 2
    gq = jnp.tile(g_q, N_Q_HEADS).reshape(1, ATTN_WIDTH)
    gk = jnp.tile(g_k, N_KV_HEADS).reshape(1, KV_WIDTH)
    const = lambda b, s: (0, 0)
    return pl.pallas_call(
        _inproj_kernel,
        grid=(B, S // ts),
        in_specs=[pl.BlockSpec((1, ts, D), lambda b, s: (b, s, 0)),
                  pl.BlockSpec((None, 1, 6, D), lambda b, s: (layer, b, 0, 0)),
                  pl.BlockSpec((1, D), const),
                  pl.BlockSpec((1, D, W), lambda b, s: (layer, 0, 0)),
                  pl.BlockSpec((1, ATTN_WIDTH), const),
                  pl.BlockSpec((1, KV_WIDTH), const),
                  pl.BlockSpec((ATTN_WIDTH, ATTN_WIDTH), const),
                  pl.BlockSpec((ts, LANES), lambda b, s: (s, 0)),
                  pl.BlockSpec((ts, LANES), lambda b, s: (s, 0))],
        out_specs=[pl.BlockSpec((1, ts, ATTN_WIDTH), lambda b, s: (b, s, 0)),
                   pl.BlockSpec((1, ts, KV_WIDTH), lambda b, s: (b, s, 0)),
                   pl.BlockSpec((1, ts, KV_WIDTH), lambda b, s: (b, s, 0)),
                   pl.BlockSpec((1, ts, cw), lambda b, s: (b, s, 0))],
        out_shape=[jax.ShapeDtypeStruct((B, S, ATTN_WIDTH), BF16),
                   jax.ShapeDtypeStruct((B, S, KV_WIDTH), BF16),
                   jax.ShapeDtypeStruct((B, S, KV_WIDTH), BF16),
                   jax.ShapeDtypeStruct((B, S, cw), F32)],
        scratch_shapes=[pltpu.VMEM((D, W), BF16)],
        compiler_params=_cparams("arbitrary", "arbitrary"),
        name="inproj",
    )(x, mod, g_mix.reshape(1, D), w_in, gq, gk, bd, cos_t, sin_t)


def _attn_kernel(q_ref, k_ref, v_ref, o_ref):
    k = k_ref[0]
    v = v_ref[0]
    tq = q_ref.shape[1]
    lane = lax.broadcasted_iota(I32, (tq, LANES), 1)
    low = lane < HEAD_DIM
    heads_per_tile = LANES // HEAD_DIM
    group = N_Q_HEADS // N_KV_HEADS
    for j in range(ATTN_WIDTH // LANES):
        qt = q_ref[0, :, j * LANES:(j + 1) * LANES].astype(F32)
        g = (j * heads_per_tile) // group
        placed = []
        for hh in range(heads_per_tile):
            qm = jnp.where(low if hh == 0 else jnp.logical_not(low), qt, 0.0)
            if hh != g:
                qm = pltpu.roll(qm, HEAD_DIM, 1)
            s = lax.dot_general(qm.astype(BF16), k, (((1,), (1,)), ((), ())),
                                preferred_element_type=F32)
            m = jnp.max(s, axis=-1, keepdims=True)
            p = jnp.exp2(s - m)
            l = jnp.sum(p, axis=-1, keepdims=True)
            o = jnp.dot(p.astype(BF16), v, preferred_element_type=F32) / l
            if hh != g:
                o = pltpu.roll(o, HEAD_DIM, 1)
            placed.append(o)
        o_ref[0, :, j * LANES:(j + 1) * LANES] = jnp.where(low, placed[0], placed[1]).astype(BF16)


def _attention(q, k, v, tq):
    B, S, _ = q.shape
    return pl.pallas_call(
        _attn_kernel,
        grid=(B, S // tq),
        in_specs=[pl.BlockSpec((1, tq, ATTN_WIDTH), lambda b, i: (b, i, 0)),
                  pl.BlockSpec((1, S, KV_WIDTH), lambda b, i: (b, 0, 0)),
                  pl.BlockSpec((1, S, KV_WIDTH), lambda b, i: (b, 0, 0))],
        out_specs=pl.BlockSpec((1, tq, ATTN_WIDTH), lambda b, i: (b, i, 0)),
        out_shape=jax.ShapeDtypeStruct((B, S, ATTN_WIDTH), BF16),
        compiler_params=_cparams("parallel", "parallel"),
        name="attention",
    )(q, k, v)


def _conv_kernel(u_ref, w_ref, b_ref, g_ref, beta_ref, o_ref, pad_ref, *, tc):
    S, C = u_ref.shape[1], u_ref.shape[2]
    halo = 2 * SUB
    pad_ref[0:halo, :] = jnp.zeros((halo, C), F32)
    pad_ref[halo + S:halo + S + halo, :] = jnp.zeros((halo, C), F32)
    pad_ref[halo:halo + S, :] = u_ref[0]
    span = tc + 2 * halo

    def tile(i, carry):
        s0 = pl.multiple_of(i * tc, tc)
        cols = []
        for c0 in range(0, C, LANES):
            win = pad_ref[pl.ds(s0, span), c0:c0 + LANES]
            acc = jnp.zeros((tc, LANES), F32)
            for r in range(SUB):
                shifted = win if r == 0 else pltpu.roll(win, span - r, 0)
                for j in range(CONV_WIDTH):
                    off = halo - CONV_PAD + j
                    if off % SUB == r:
                        a = off - r
                        acc = acc + shifted[a:a + tc, :] * w_ref[j:j + 1, c0:c0 + LANES]
            cols.append(acc)
        acc = jnp.concatenate(cols, axis=1) + b_ref[...]
        mu = jnp.mean(acc, axis=-1, keepdims=True)
        d = acc - mu
        var = jnp.mean(d * d, axis=-1, keepdims=True)
        y = d * lax.rsqrt(var + EPS) * g_ref[...] + beta_ref[...]
        o_ref[0, pl.ds(s0, tc), :] = (y * jax.nn.sigmoid(y)).astype(BF16)
        return carry

    lax.fori_loop(0, S // tc, tile, 0)


def _conv(u, w_dw, b_dw, g_cn, b_cn, tc=64):
    B, S, C = u.shape
    const = lambda b: (0, 0)
    return pl.pallas_call(
        functools.partial(_conv_kernel, tc=tc),
        grid=(B,),
        in_specs=[pl.BlockSpec((1, S, C), lambda b: (b, 0, 0)),
                  pl.BlockSpec((CONV_WIDTH, C), const),
                  pl.BlockSpec((1, C), const),
                  pl.BlockSpec((1, C), const),
                  pl.BlockSpec((1, C), const)],
        out_specs=pl.BlockSpec((1, S, C), lambda b: (b, 0, 0)),
        out_shape=jax.ShapeDtypeStruct((B, S, C), BF16),
        scratch_shapes=[pltpu.VMEM((S + 4 * SUB, C), F32)],
        compiler_params=_cparams("parallel"),
        name="conv",
    )(u, w_dw, b_dw.reshape(1, C), g_cn.reshape(1, C), b_cn.reshape(1, C))


def _split_bf16(a):
    hi = a.astype(BF16)
    lo = (a - hi.astype(F32)).astype(BF16)
    return hi, lo


def _outproj_router_kernel(x_ref, attn_ref, conv_ref, wout_ref, mod_ref, gffn_ref, wr_ref, br_ref, tri_ref, etri_ref,
                           x1_ref, h2_ref, pos_ref, gp_ref, a8_ref, n8_ref, b8_ref, tot_ref, carry_ref, wb_ref):
    first_step = jnp.logical_and(pl.program_id(0) == 0, pl.program_id(1) == 0)

    @pl.when(first_step)
    def _():
        carry_ref[...] = jnp.zeros_like(carry_ref)
        wb_ref[...] = wout_ref[0].astype(BF16)

    mod = mod_ref[0]
    gt1, sh2, sc2 = mod[2:3], mod[3:4], mod[4:5]
    aw = attn_ref.shape[2]
    mix = (jnp.dot(attn_ref[0], wb_ref[:aw, :], preferred_element_type=F32)
           + jnp.dot(conv_ref[0], wb_ref[aw:, :], preferred_element_type=F32))
    x1 = x_ref[0] + gt1 * mix
    x1_ref[0] = x1
    ms = jnp.mean(x1 * x1, axis=-1, keepdims=True)
    h2 = x1 * lax.rsqrt(ms + EPS) * gffn_ref[...] * (1.0 + sc2) + sh2
    h2_ref[...] = h2.astype(BF16)

    h_hi, h_lo = _split_bf16(h2)
    w_hi, w_lo = _split_bf16(wr_ref[...])
    nt = (((1,), (1,)), ((), ()))
    logits = (lax.dot_general(w_hi, h_hi, nt, preferred_element_type=F32)
              + lax.dot_general(w_lo, h_hi, nt, preferred_element_type=F32)
              + lax.dot_general(w_hi, h_lo, nt, preferred_element_type=F32)) + br_ref[...]
    n_exp, ts = logits.shape
    reps = ts // LANES
    eio = lax.broadcasted_iota(I32, logits.shape, 0).astype(F32)
    vals, hits = [], []
    work = logits
    for _ in range(TOP_K):
        m = jnp.max(work, axis=0, keepdims=True)
        ik = jnp.min(jnp.where(work == m, eio, float(n_exp)), axis=0, keepdims=True)
        hit = eio == ik
        work = jnp.where(hit, -jnp.inf, work)
        vals.append(m)
        hits.append(hit)
    ex = [jnp.exp(v - vals[0]) for v in vals]
    den = ex[0] + ex[1] + ex[2] + ex[3]

    chosen = jnp.zeros(logits.shape, F32)
    for hit in hits:
        chosen = chosen + hit.astype(F32)
    prefix = jnp.dot(chosen.astype(BF16), tri_ref[...], preferred_element_type=F32)
    cnt = jnp.sum(chosen, axis=1, keepdims=True)
    n8 = jnp.broadcast_to(jnp.floor((cnt + (SUB - 1.0)) * (1.0 / SUB)), (n_exp, LANES))
    b8 = jnp.dot(etri_ref[...], n8.astype(BF16), preferred_element_type=F32)
    slot = jnp.concatenate([b8 * float(SUB)] * reps, axis=1) + prefix
    pos = [jnp.sum(jnp.where(hit, slot, 0.0), axis=0, keepdims=True) for hit in hits]
    for kk in range(TOP_K):
        pos_ref[kk:kk + 1, :] = pos[kk].astype(I32)
    a8_ref[0] = carry_ref[...].astype(I32)
    n8_ref[0] = n8.astype(I32)
    b8_ref[0] = b8.astype(I32)
    carry_ref[...] = carry_ref[...] + n8
    tot_ref[...] = carry_ref[...]

    rows = [e / den for e in ex] + pos + [jnp.zeros((LANES - 2 * TOP_K, ts), F32)]
    gp_ref[...] = jnp.concatenate(rows, axis=0).T


def _outproj_router(x, attn, conv, w_out, mod, layer, g_ffn, w_router, b_router, tri, etri, ts):
    B, S, D = x.shape
    N = B * S
    E = w_router.shape[1]
    nts = S // ts
    aw, cw = attn.shape[2], conv.shape[2]
    const = lambda b, s: (0, 0)
    tok = lambda b, s: (0, b * nts + s)
    tile3 = lambda b, s: (b * nts + s, 0, 0)
    return pl.pallas_call(
        _outproj_router_kernel,
        grid=(B, nts),
        in_specs=[pl.BlockSpec((1, ts, D), lambda b, s: (b, s, 0)),
                  pl.BlockSpec((1, ts, aw), lambda b, s: (b, s, 0)),
                  pl.BlockSpec((1, ts, cw), lambda b, s: (b, s, 0)),
                  pl.BlockSpec((1, aw + cw, D), lambda b, s: (layer, 0, 0)),
                  pl.BlockSpec((None, 1, 6, D), lambda b, s: (layer, b, 0, 0)),
                  pl.BlockSpec((1, D), const),
                  pl.BlockSpec((E, D), const),
                  pl.BlockSpec((E, 1), const),
                  pl.BlockSpec((ts, ts), const),
                  pl.BlockSpec((E, E), const)],
        out_specs=[pl.BlockSpec((1, ts, D), lambda b, s: (b, s, 0)),
                   pl.BlockSpec((ts, D), lambda b, s: (b * nts + s, 0)),
                   pl.BlockSpec((TOP_K, ts), tok),
                   pl.BlockSpec((ts, LANES), lambda b, s: (b * nts + s, 0)),
                   pl.BlockSpec((1, E, LANES), tile3),
                   pl.BlockSpec((1, E, LANES), tile3),
                   pl.BlockSpec((1, E, LANES), tile3),
                   pl.BlockSpec((E, LANES), const)],
        out_shape=[jax.ShapeDtypeStruct((B, S, D), F32),
                   jax.ShapeDtypeStruct((N, D), BF16),
                   jax.ShapeDtypeStruct((TOP_K, N), I32),
                   jax.ShapeDtypeStruct((N, LANES), F32),
                   jax.ShapeDtypeStruct((B * nts, E, LANES), I32),
                   jax.ShapeDtypeStruct((B * nts, E, LANES), I32),
                   jax.ShapeDtypeStruct((B * nts, E, LANES), I32),
                   jax.ShapeDtypeStruct((E, LANES), F32)],
        scratch_shapes=[pltpu.VMEM((E, LANES), F32), pltpu.VMEM((aw + cw, D), BF16)],
        compiler_params=_cparams("arbitrary", "arbitrary"),
        name="outproj_router",
    )(x, attn, conv, w_out, mod, g_ffn.reshape(1, D), w_router.T, b_router.reshape(E, 1), tri, etri)


def _segment_copies(n8_ref, a_ref, b_ref, base, n_exp, copy, wait):
    for e in range(n_exp):
        n = n8_ref[base + e]
        a0 = a_ref[base + e]
        b0 = b_ref[base + e]

        def body(u, c):
            d = copy(pl.multiple_of((a0 + u) * SUB, SUB), pl.multiple_of((b0 + u) * SUB, SUB))
            if wait:
                d.wait()
            else:
                d.start()
            return c

        lax.fori_loop(0, n, body, 0)


def _dispatch_kernel(dst8_ref, n8_ref, src8_ref, tail0_ref, tailn_ref, nu_ref,
                     pos_ref, h_ref, xs_ref, stage_ref, zero_ref, sem, zsem, *, chunk):
    j = pl.program_id(0)
    ts = h_ref.shape[0]
    n_exp = tail0_ref.shape[0]
    rows = stage_ref.shape[0]
    h = h_ref[...]
    pos = pos_ref[...]
    for r0 in range(0, rows, chunk):
        rio = lax.broadcasted_iota(I32, (chunk, ts), 0) + r0
        onehot = jnp.zeros((chunk, ts), F32)
        for kk in range(TOP_K):
            onehot = jnp.where(rio == pos[kk:kk + 1, :], 1.0, onehot)
        stage_ref[r0:r0 + chunk, :] = jnp.dot(onehot.astype(BF16), h, preferred_element_type=F32)

    def seg_copy(src, dst):
        return pltpu.make_async_copy(stage_ref.at[pl.ds(src, SUB)], xs_ref.at[pl.ds(dst, SUB)], sem)

    _segment_copies(n8_ref, src8_ref, dst8_ref, j * n_exp, n_exp, seg_copy, wait=False)

    @pl.when(j == pl.num_programs(0) - 1)
    def _():
        zero_ref[...] = jnp.zeros_like(zero_ref)

        def zero_copy(u):
            return pltpu.make_async_copy(zero_ref.at[pl.ds(0, SUB)],
                                         xs_ref.at[pl.ds(pl.multiple_of(u * SUB, SUB), SUB)], zsem)

        for e in range(n_exp):
            lo = tail0_ref[e]
            hi = lo + tailn_ref[e]
            lax.fori_loop(lo, hi, lambda u, c: (zero_copy(u).start(), c)[1], 0)
            lax.fori_loop(lo, hi, lambda u, c: (zero_copy(u).wait(), c)[1], 0)

        tm = zero_ref.shape[0]

        def zero_block(b):
            return pltpu.make_async_copy(zero_ref, xs_ref.at[pl.ds(pl.multiple_of(b * tm, tm), tm)], zsem)

        n_blocks = xs_ref.shape[0] // tm
        lax.fori_loop(nu_ref[0], n_blocks, lambda b, c: (zero_block(b).start(), c)[1], 0)
        lax.fori_loop(nu_ref[0], n_blocks, lambda b, c: (zero_block(b).wait(), c)[1], 0)

    _segment_copies(n8_ref, src8_ref, dst8_ref, j * n_exp, n_exp, seg_copy, wait=True)


def _dispatch(dst8, n8, src8, tail0, tailn, n_used, pos, h2, n_rows, ts, tm):
    N, D = h2.shape
    E = tail0.shape[0]
    rows = _stage_rows(ts, E)
    chunk = 256 if rows % 256 == 0 else LANES
    return pl.pallas_call(
        functools.partial(_dispatch_kernel, chunk=chunk),
        grid_spec=pltpu.PrefetchScalarGridSpec(
            num_scalar_prefetch=6,
            grid=(N // ts,),
            in_specs=[pl.BlockSpec((TOP_K, ts), lambda i, *_: (0, i)),
                      pl.BlockSpec((ts, D), lambda i, *_: (i, 0))],
            out_specs=pl.BlockSpec(memory_space=pl.ANY),
            scratch_shapes=[pltpu.VMEM((rows, D), F32),
                            pltpu.VMEM((tm, D), F32),
                            pltpu.SemaphoreType.DMA,
                            pltpu.SemaphoreType.DMA]),
        out_shape=jax.ShapeDtypeStruct((n_rows, D), F32),
        compiler_params=_cparams("arbitrary"),
        name="dispatch",
    )(dst8, n8, src8, tail0, tailn, n_used, pos, h2)


def _expert_kernel(be_ref, nu_ref, x_ref, w1_ref, b1_ref, w2_ref, b2_ref, y_ref, w1b_ref, w2b_ref):
    i = pl.program_id(0)
    live = i < nu_ref[0]
    fresh = jnp.logical_or(i == 0, be_ref[i] != be_ref[jnp.maximum(i - 1, 0)])

    @pl.when(jnp.logical_and(live, fresh))
    def _():
        w1b_ref[...] = w1_ref[0, 0].astype(BF16)
        w2b_ref[...] = w2_ref[0, 0].astype(BF16)

    @pl.when(live)
    def _():
        dx = w2b_ref.shape[0]
        hu = jnp.dot(x_ref[...].astype(BF16), w1b_ref[...], preferred_element_type=F32) + b1_ref[0, 0]
        x_glu = jnp.minimum(hu[:, :dx], SWIGLU_LIMIT)
        x_lin = jnp.clip(hu[:, dx:], -SWIGLU_LIMIT, SWIGLU_LIMIT)
        act = (x_lin + 1.0) * (x_glu * jax.nn.sigmoid(SWIGLU_ALPHA * x_glu))
        y_ref[...] = jnp.dot(act.astype(BF16), w2b_ref[...], preferred_element_type=F32) + b2_ref[0, 0]

    @pl.when(jnp.logical_not(live))
    def _():
        y_ref[...] = jnp.zeros_like(y_ref)


def _experts(block_e, n_used, xs, w1, b1, w2, b2, layer, tm):
    P, D = xs.shape
    _, E, _, H = w1.shape
    dx = w2.shape[2]
    row = lambda i, be, nu: (jnp.minimum(i, nu[0] - 1), 0)
    exp4 = lambda i, be, nu: (layer, be[i], 0, 0)
    return pl.pallas_call(
        _expert_kernel,
        grid_spec=pltpu.PrefetchScalarGridSpec(
            num_scalar_prefetch=2,
            grid=(P // tm,),
            in_specs=[pl.BlockSpec((tm, D), row),
                      pl.BlockSpec((1, 1, D, H), exp4),
                      pl.BlockSpec((1, 1, 1, H), exp4),
                      pl.BlockSpec((1, 1, dx, D), exp4),
                      pl.BlockSpec((1, 1, 1, D), exp4)],
            out_specs=pl.BlockSpec((tm, D), lambda i, be, nu: (i, 0)),
            scratch_shapes=[pltpu.VMEM((D, H), BF16), pltpu.VMEM((dx, D), BF16)]),
        out_shape=jax.ShapeDtypeStruct((P, D), F32),
        compiler_params=_cparams("arbitrary"),
        name="experts",
    )(block_e, n_used, xs, w1, b1.reshape(b1.shape[0], E, 1, H), w2, b2.reshape(b2.shape[0], E, 1, D))


def _combine_kernel(dst8_ref, n8_ref, src8_ref,
                    x_ref, mod_ref, gp_ref, gfin_ref, ys_ref, o_ref, stage_ref, sem, *, final, chunk):
    j = pl.program_id(0) * pl.num_programs(1) + pl.program_id(1)
    ts = x_ref.shape[1]
    rows = stage_ref.shape[0]
    n_exp = (rows - ts * TOP_K) // SUB

    def seg_copy(src, dst):
        return pltpu.make_async_copy(ys_ref.at[pl.ds(dst, SUB)], stage_ref.at[pl.ds(src, SUB)], sem)

    stage_ref[ts * TOP_K:rows, :] = jnp.zeros((rows - ts * TOP_K, stage_ref.shape[1]), F32)
    _segment_copies(n8_ref, src8_ref, dst8_ref, j * n_exp, n_exp, seg_copy, wait=False)
    gp = gp_ref[...]
    gates = [jnp.broadcast_to(gp[:, kk:kk + 1], (ts, chunk)) for kk in range(TOP_K)]
    slots = [jnp.broadcast_to(gp[:, TOP_K + kk:TOP_K + kk + 1].astype(I32), (ts, chunk)) for kk in range(TOP_K)]
    _segment_copies(n8_ref, src8_ref, dst8_ref, j * n_exp, n_exp, seg_copy, wait=True)

    moe = jnp.zeros((ts, stage_ref.shape[1]), F32)
    for r0 in range(0, rows, chunk):
        rio = lax.broadcasted_iota(I32, (ts, chunk), 1) + r0
        wgt = jnp.zeros((ts, chunk), F32)
        for kk in range(TOP_K):
            wgt = jnp.where(rio == slots[kk], gates[kk], wgt)
        moe = moe + jnp.dot(wgt.astype(BF16), stage_ref[r0:r0 + chunk, :].astype(BF16),
                            preferred_element_type=F32)
    gt2 = mod_ref[0][5:6]
    out = x_ref[0] + gt2 * moe
    if final:
        ms = jnp.mean(out * out, axis=-1, keepdims=True)
        out = out * lax.rsqrt(ms + EPS) * gfin_ref[...]
    o_ref[0] = out


def _combine(dst8, n8, src8, x1, mod, layer, gp, g_final, ys, n_exp, ts, final):
    B, S, D = x1.shape
    nts = S // ts
    rows = _stage_rows(ts, n_exp)
    chunk = 256 if rows % 256 == 0 else LANES
    return pl.pallas_call(
        functools.partial(_combine_kernel, final=final, chunk=chunk),
        grid_spec=pltpu.PrefetchScalarGridSpec(
            num_scalar_prefetch=3,
            grid=(B, nts),
            in_specs=[pl.BlockSpec((1, ts, D), lambda b, s, *_: (b, s, 0)),
                      pl.BlockSpec((None, 1, 6, D), lambda b, s, *_: (layer, b, 0, 0)),
                      pl.BlockSpec((ts, LANES), lambda b, s, *_: (b * nts + s, 0)),
                      pl.BlockSpec((1, D), lambda b, s, *_: (0, 0)),
                      pl.BlockSpec(memory_space=pl.ANY)],
            out_specs=pl.BlockSpec((1, ts, D), lambda b, s, *_: (b, s, 0)),
            scratch_shapes=[pltpu.VMEM((rows, D), F32), pltpu.SemaphoreType.DMA]),
        out_shape=jax.ShapeDtypeStruct((B, S, D), F32),
        compiler_params=_cparams("arbitrary", "arbitrary"),
        name="combine",
    )(dst8, n8, src8, x1, mod, gp, g_final.reshape(1, D), ys)


def _rope_tables(seq_len):
    freqs = ROPE_THETA ** (-jnp.arange(ROPE_FREQS, dtype=F32) / ROPE_FREQS)
    pos = jnp.arange(seq_len, dtype=I32)
    row_ang = (pos // GRID_W).astype(F32)[:, None] * freqs
    col_ang = (pos % GRID_W).astype(F32)[:, None] * freqs
    cos = jnp.concatenate([jnp.cos(row_ang)] * 2 + [jnp.cos(col_ang)] * 2, axis=1)
    sin = jnp.concatenate([-jnp.sin(row_ang), jnp.sin(row_ang), -jnp.sin(col_ang), jnp.sin(col_ang)], axis=1)
    reps = LANES // HEAD_DIM
    return jnp.tile(cos, (1, reps)), jnp.tile(sin, (1, reps))


def _tile_rows(n, prefs):
    for t in prefs:
        if n % t == 0:
            return t
    return n


def kernel(x, c, w_mod, b_mod, g_mix, w_in, g_q, g_k, w_dw, b_dw, g_cn, b_cn,
           w_out, g_ffn, w_router, b_router, w1, b1, w2, b2, g_final):
    B, S, D = x.shape
    L = w_mod.shape[0]
    E = w_router.shape[2]
    N = B * S
    ts = _tile_rows(S, (512, 256, 128))
    tq = _tile_rows(S, (256, 128))
    tm = 512
    n_tiles = N // ts
    max_rows = N * TOP_K + n_tiles * E * (SUB - 1)
    n_rows = (max_rows + tm - 1) // tm * tm + E * tm
    n_blocks = n_rows // tm

    mod = _modulation(c, w_mod, b_mod).reshape(L, B, 6, D)
    cos_t, sin_t = _rope_tables(S)
    head = jnp.arange(ATTN_WIDTH, dtype=I32) // HEAD_DIM
    bd = jnp.where(head[:, None] == head[None, :], 1.0 / HEAD_DIM, 0.0).astype(BF16)
    tok = jnp.arange(ts, dtype=I32)
    tri = (tok[:, None] < tok[None, :]).astype(BF16)
    eid = jnp.arange(E, dtype=I32)
    etri = (eid[None, :] < eid[:, None]).astype(BF16)

    for l in range(L):
        q, k, v, u = _inproj(x, mod, l, g_mix[l], w_in, g_q[l], g_k[l], bd, cos_t, sin_t, ts)
        attn = _attention(q, k, v, tq)
        conv = _conv(u, w_dw[l], b_dw[l], g_cn[l], b_cn[l])
        x1, h2, pos, gp, a8, n8, b8, tot = _outproj_router(
            x, attn, conv, w_out, mod, l, g_ffn[l], w_router[l], b_router[l], tri, etri, ts)

        rows_e = tot[:, 0].astype(I32) * SUB
        padded = (rows_e + tm - 1) // tm * tm
        pend = jnp.cumsum(padded)
        pstart = pend - padded
        n_used = (pend[-1] // tm).astype(I32)
        blk = jnp.minimum(jnp.arange(n_blocks, dtype=I32), n_used - 1) * tm
        block_e = jnp.minimum(jnp.sum(blk[:, None] >= pend[None, :], axis=1), E - 1).astype(I32)
        dst8 = (a8[:, :, 0] + (pstart // SUB)[None, :]).reshape(-1)
        n8f = n8[:, :, 0].reshape(-1)
        src8 = b8[:, :, 0].reshape(-1)
        tail0 = (pstart + rows_e) // SUB
        tailn = (padded - rows_e) // SUB

        n_used = n_used.reshape(1)
        xs = _dispatch(dst8, n8f, src8, tail0, tailn, n_used, pos, h2, n_rows, ts, tm)
        ys = _experts(block_e, n_used, xs, w1, b1, w2, b2, l, tm)
        x = _combine(dst8, n8f, src8, x1, mod, l, gp, g_final, ys, E, ts, final=(l == L - 1))
    return x
```

```python
import functools

import jax
import jax.numpy as jnp
from jax import lax
from jax.experimental import pallas as pl
from jax.experimental.pallas import tpu as pltpu

F32 = jnp.float32
BF16 = jnp.bfloat16
I32 = jnp.int32

HEAD_DIM = 64
N_Q_HEADS = 8
N_KV_HEADS = 2
ATTN_WIDTH = N_Q_HEADS * HEAD_DIM
KV_WIDTH = N_KV_HEADS * HEAD_DIM
GRID_W = 64
ROPE_THETA = 10000.0
ROPE_FREQS = HEAD_DIM // 4
CONV_WIDTH = 31
CONV_PAD = CONV_WIDTH // 2
TOP_K = 4
SWIGLU_LIMIT = 7.0
SWIGLU_ALPHA = 1.702
EPS = 1e-6
LOG2E = 1.4426950408889634

LANES = 128
SUB = 8
VMEM_LIMIT = 56 * 1024 * 1024


def _cparams(*sem):
    return pltpu.CompilerParams(dimension_semantics=sem, vmem_limit_bytes=VMEM_LIMIT)


def _stage_rows(ts, n_exp):
    return ts * TOP_K + n_exp * SUB


def _mod_kernel(c_ref, w_ref, b_ref, o_ref):
    c = c_ref[...]
    c_act = (c * jax.nn.sigmoid(c)).astype(BF16)
    o_ref[0] = jnp.dot(c_act, w_ref[0].astype(BF16), preferred_element_type=F32) + b_ref[0]


def _modulation(c, w_mod, b_mod):
    L, D, W = w_mod.shape
    B = c.shape[0]
    tn = 1536
    return pl.pallas_call(
        _mod_kernel,
        grid=(L, W // tn),
        in_specs=[pl.BlockSpec((B, D), lambda l, j: (0, 0)),
                  pl.BlockSpec((1, D, tn), lambda l, j: (l, 0, j)),
                  pl.BlockSpec((1, 1, tn), lambda l, j: (l, 0, j))],
        out_specs=pl.BlockSpec((1, B, tn), lambda l, j: (l, 0, j)),
        out_shape=jax.ShapeDtypeStruct((L, B, W), F32),
        compiler_params=_cparams("parallel", "parallel"),
        name="modulation",
    )(c, w_mod, b_mod.reshape(L, 1, W))


def _head_rmsnorm(t, gain, bd):
    msq = jnp.dot((t * t).astype(BF16), bd, preferred_element_type=F32)
    return t * lax.rsqrt(msq + EPS) * gain


def _rope(t, cos, sin):
    w = t.shape[1]
    reps = w // LANES
    cosw = jnp.concatenate([cos] * reps, axis=1) if reps > 1 else cos
    sinw = jnp.concatenate([sin] * reps, axis=1) if reps > 1 else sin
    lane = lax.broadcasted_iota(I32, t.shape, 1)
    first = (lane & ROPE_FREQS) == 0
    partner = jnp.where(first, pltpu.roll(t, w - ROPE_FREQS, 1), pltpu.roll(t, ROPE_FREQS, 1))
    return t * cosw + partner * sinw


def _inproj_kernel(x_ref, mod_ref, gmix_ref, w_ref, gq_ref, gk_ref, bd_ref, cos_ref, sin_ref,
                   q_ref, k_ref, v_ref, u_ref, wb_ref):
    @pl.when(jnp.logical_and(pl.program_id(0) == 0, pl.program_id(1) == 0))
    def _():
        wb_ref[...] = w_ref[0].astype(BF16)

    x = x_ref[0]
    mod = mod_ref[0]
    sh1, sc1 = mod[0:1], mod[1:2]
    ms = jnp.mean(x * x, axis=-1, keepdims=True)
    xn = x * lax.rsqrt(ms + EPS) * gmix_ref[...]
    h = (xn * (1.0 + sc1) + sh1).astype(BF16)
    proj = jnp.dot(h, wb_ref[...], preferred_element_type=F32)
    a0, a1, a2 = ATTN_WIDTH, ATTN_WIDTH + KV_WIDTH, ATTN_WIDTH + 2 * KV_WIDTH
    cw = (proj.shape[1] - a2) // 2
    q, k, v = proj[:, :a0], proj[:, a0:a1], proj[:, a1:a2]
    ca, cg = proj[:, a2:a2 + cw], proj[:, a2 + cw:]
    cos, sin = cos_ref[...], sin_ref[...]
    bd = bd_ref[...]
    q = _rope(_head_rmsnorm(q, gq_ref[...], bd), cos, sin)
    k = _rope(_head_rmsnorm(k, gk_ref[...], bd[:KV_WIDTH, :KV_WIDTH]), cos, sin)
    q_ref[0] = (q * (HEAD_DIM ** -0.5 * LOG2E)).astype(BF16)
    k_ref[0] = k.astype(BF16)
    v_ref[0] = v.astype(BF16)
    u_ref[0] = ca * jax.nn.sigmoid(cg)


def _inproj(x, mod, layer, g_mix, w_in, g_q, g_k, bd, cos_t, sin_t, ts):
    B, S, D = x.shape
    W = w_in.shape[2]
    cw = (W - ATTN_WIDTH - 2 * KV_WIDTH) // 2
    gq = jnp.tile(g_q, N_Q_HEADS).reshape(1, ATTN_WIDTH)
    gk = jnp.tile(g_k, N_KV_HEADS).reshape(1, KV_WIDTH)
    const = lambda b, s: (0, 0)
    return pl.pallas_call(
        _inproj_kernel,
        grid=(B, S // ts),
        in_specs=[pl.BlockSpec((1, ts, D), lambda b, s: (b, s, 0)),
                  pl.BlockSpec((None, 1, 6, D), lambda b, s: (layer, b, 0, 0)),
                  pl.BlockSpec((1, D), const),
                  pl.BlockSpec((1, D, W), lambda b, s: (layer, 0, 0)),
                  pl.BlockSpec((1, ATTN_WIDTH), const),
                  pl.BlockSpec((1, KV_WIDTH), const),
                  pl.BlockSpec((ATTN_WIDTH, ATTN_WIDTH), const),
                  pl.BlockSpec((ts, LANES), lambda b, s: (s, 0)),
                  pl.BlockSpec((ts, LANES), lambda b, s: (s, 0))],
        out_specs=[pl.BlockSpec((1, ts, ATTN_WIDTH), lambda b, s: (b, s, 0)),
                   pl.BlockSpec((1, ts, KV_WIDTH), lambda b, s: (b, s, 0)),
                   pl.BlockSpec((1, ts, KV_WIDTH), lambda b, s: (b, s, 0)),
                   pl.BlockSpec((1, ts, cw), lambda b, s: (b, s, 0))],
        out_shape=[jax.ShapeDtypeStruct((B, S, ATTN_WIDTH), BF16),
                   jax.ShapeDtypeStruct((B, S, KV_WIDTH), BF16),
                   jax.ShapeDtypeStruct((B, S, KV_WIDTH), BF16),
                   jax.ShapeDtypeStruct((B, S, cw), F32)],
        scratch_shapes=[pltpu.VMEM((D, W), BF16)],
        compiler_params=_cparams("arbitrary", "arbitrary"),
        name="inproj",
    )(x, mod, g_mix.reshape(1, D), w_in, gq, gk, bd, cos_t, sin_t)


def _attn_kernel(q_ref, k_ref, v_ref, o_ref):
    k = k_ref[0]
    v = v_ref[0]
    tq = q_ref.shape[1]
    lane = lax.broadcasted_iota(I32, (tq, LANES), 1)
    low = lane < HEAD_DIM
    heads_per_tile = LANES // HEAD_DIM
    group = N_Q_HEADS // N_KV_HEADS

    def scores(h):
        j, hh = divmod(h, heads_per_tile)
        g = h // group
        qt = q_ref[0, :, j * LANES:(j + 1) * LANES].astype(F32)
        qm = jnp.where(low if hh == 0 else jnp.logical_not(low), qt, 0.0)
        if hh != g:
            qm = pltpu.roll(qm, HEAD_DIM, 1)
        return lax.dot_general(qm.astype(BF16), k, (((1,), (1,)), ((), ())),
                               preferred_element_type=F32)

    s_next = scores(0)
    placed = []
    for h in range(N_Q_HEADS):
        s = s_next
        if h + 1 < N_Q_HEADS:
            s_next = scores(h + 1)
        j, hh = divmod(h, heads_per_tile)
        m = jnp.max(s, axis=-1, keepdims=True)
        p = jnp.exp2(s - m)
        l = jnp.sum(p, axis=-1, keepdims=True)
        o = jnp.dot(p.astype(BF16), v, preferred_element_type=F32) / l
        if hh != h // group:
            o = pltpu.roll(o, HEAD_DIM, 1)
        placed.append(o)
        if hh == heads_per_tile - 1:
            o_ref[0, :, j * LANES:(j + 1) * LANES] = jnp.where(low, placed[0], placed[1]).astype(BF16)
            placed = []


def _attention(q, k, v, tq):
    B, S, _ = q.shape
    return pl.pallas_call(
        _attn_kernel,
        grid=(B, S // tq),
        in_specs=[pl.BlockSpec((1, tq, ATTN_WIDTH), lambda b, i: (b, i, 0)),
                  pl.BlockSpec((1, S, KV_WIDTH), lambda b, i: (b, 0, 0)),
                  pl.BlockSpec((1, S, KV_WIDTH), lambda b, i: (b, 0, 0))],
        out_specs=pl.BlockSpec((1, tq, ATTN_WIDTH), lambda b, i: (b, i, 0)),
        out_shape=jax.ShapeDtypeStruct((B, S, ATTN_WIDTH), BF16),
        compiler_params=_cparams("parallel", "parallel"),
        name="attention",
    )(q, k, v)


def _conv_kernel(u_ref, w_ref, b_ref, g_ref, beta_ref, o_ref, pad_ref, *, tc):
    S, C = u_ref.shape[1], u_ref.shape[2]
    halo = 2 * SUB
    pad_ref[0:halo, :] = jnp.zeros((halo, C), F32)
    pad_ref[halo + S:halo + S + halo, :] = jnp.zeros((halo, C), F32)
    pad_ref[halo:halo + S, :] = u_ref[0]
    span = tc + 2 * halo

    def tile(i, carry):
        s0 = pl.multiple_of(i * tc, tc)
        cols = []
        for c0 in range(0, C, LANES):
            win = pad_ref[pl.ds(s0, span), c0:c0 + LANES]
            acc = jnp.zeros((tc, LANES), F32)
            for r in range(SUB):
                shifted = win if r == 0 else pltpu.roll(win, span - r, 0)
                for j in range(CONV_WIDTH):
                    off = halo - CONV_PAD + j
                    if off % SUB == r:
                        a = off - r
                        acc = acc + shifted[a:a + tc, :] * w_ref[j:j + 1, c0:c0 + LANES]
            cols.append(acc)
        acc = jnp.concatenate(cols, axis=1) + b_ref[...]
        mu = jnp.mean(acc, axis=-1, keepdims=True)
        d = acc - mu
        var = jnp.mean(d * d, axis=-1, keepdims=True)
        y = d * lax.rsqrt(var + EPS) * g_ref[...] + beta_ref[...]
        o_ref[0, pl.ds(s0, tc), :] = (y * jax.nn.sigmoid(y)).astype(BF16)
        return carry

    lax.fori_loop(0, S // tc, tile, 0)


def _conv(u, w_dw, b_dw, g_cn, b_cn, tc=64):
    B, S, C = u.shape
    const = lambda b: (0, 0)
    return pl.pallas_call(
        functools.partial(_conv_kernel, tc=tc),
        grid=(B,),
        in_specs=[pl.BlockSpec((1, S, C), lambda b: (b, 0, 0)),
                  pl.BlockSpec((CONV_WIDTH, C), const),
                  pl.BlockSpec((1, C), const),
                  pl.BlockSpec((1, C), const),
                  pl.BlockSpec((1, C), const)],
        out_specs=pl.BlockSpec((1, S, C), lambda b: (b, 0, 0)),
        out_shape=jax.ShapeDtypeStruct((B, S, C), BF16),
        scratch_shapes=[pltpu.VMEM((S + 4 * SUB, C), F32)],
        compiler_params=_cparams("parallel"),
        name="conv",
    )(u, w_dw, b_dw.reshape(1, C), g_cn.reshape(1, C), b_cn.reshape(1, C))


def _split_bf16(a):
    hi = a.astype(BF16)
    lo = (a - hi.astype(F32)).astype(BF16)
    return hi, lo


def _outproj_router_kernel(x_ref, attn_ref, conv_ref, wout_ref, mod_ref, gffn_ref, wr_ref, br_ref, tri_ref, etri_ref,
                           x1_ref, h2_ref, pos_ref, gp_ref, a8_ref, n8_ref, b8_ref, tot_ref, carry_ref, wb_ref):
    first_step = jnp.logical_and(pl.program_id(0) == 0, pl.program_id(1) == 0)

    @pl.when(first_step)
    def _():
        carry_ref[...] = jnp.zeros_like(carry_ref)
        wb_ref[...] = wout_ref[0].astype(BF16)

    mod = mod_ref[0]
    gt1, sh2, sc2 = mod[2:3], mod[3:4], mod[4:5]
    aw = attn_ref.shape[2]
    mix = (jnp.dot(attn_ref[0], wb_ref[:aw, :], preferred_element_type=F32)
           + jnp.dot(conv_ref[0], wb_ref[aw:, :], preferred_element_type=F32))
    x1 = x_ref[0] + gt1 * mix
    x1_ref[0] = x1
    ms = jnp.mean(x1 * x1, axis=-1, keepdims=True)
    h2 = x1 * lax.rsqrt(ms + EPS) * gffn_ref[...] * (1.0 + sc2) + sh2
    h2_ref[...] = h2.astype(BF16)

    h_hi, h_lo = _split_bf16(h2)
    w_hi, w_lo = _split_bf16(wr_ref[...])
    nt = (((1,), (1,)), ((), ()))
    logits = (lax.dot_general(w_hi, h_hi, nt, preferred_element_type=F32)
              + lax.dot_general(w_lo, h_hi, nt, preferred_element_type=F32)
              + lax.dot_general(w_hi, h_lo, nt, preferred_element_type=F32)) + br_ref[...]
    n_exp, ts = logits.shape
    reps = ts // LANES
    eio = lax.broadcasted_iota(I32, logits.shape, 0).astype(F32)
    vals, hits = [], []
    work = logits
    for _ in range(TOP_K):
        m = jnp.max(work, axis=0, keepdims=True)
        ik = jnp.min(jnp.where(work == m, eio, float(n_exp)), axis=0, keepdims=True)
        hit = eio == ik
        work = jnp.where(hit, -jnp.inf, work)
        vals.append(m)
        hits.append(hit)
    ex = [jnp.exp(v - vals[0]) for v in vals]
    den = ex[0] + ex[1] + ex[2] + ex[3]

    chosen = jnp.zeros(logits.shape, F32)
    for hit in hits:
        chosen = chosen + hit.astype(F32)
    prefix = jnp.dot(chosen.astype(BF16), tri_ref[...], preferred_element_type=F32)
    cnt = jnp.sum(chosen, axis=1, keepdims=True)
    n8 = jnp.broadcast_to(jnp.floor((cnt + (SUB - 1.0)) * (1.0 / SUB)), (n_exp, LANES))
    b8 = jnp.dot(etri_ref[...], n8.astype(BF16), preferred_element_type=F32)
    slot = jnp.concatenate([b8 * float(SUB)] * reps, axis=1) + prefix
    pos = [jnp.sum(jnp.where(hit, slot, 0.0), axis=0, keepdims=True) for hit in hits]
    for kk in range(TOP_K):
        pos_ref[kk:kk + 1, :] = pos[kk].astype(I32)
    a8_ref[0] = carry_ref[...].astype(I32)
    n8_ref[0] = n8.astype(I32)
    b8_ref[0] = b8.astype(I32)
    carry_ref[...] = carry_ref[...] + n8
    tot_ref[...] = carry_ref[...]

    rows = [e / den for e in ex] + pos + [jnp.zeros((LANES - 2 * TOP_K, ts), F32)]
    gp_ref[...] = jnp.concatenate(rows, axis=0).T


def _outproj_router(x, attn, conv, w_out, mod, layer, g_ffn, w_router, b_router, tri, etri, ts):
    B, S, D = x.shape
    N = B * S
    E = w_router.shape[1]
    nts = S // ts
    aw, cw = attn.shape[2], conv.shape[2]
    const = lambda b, s: (0, 0)
    tok = lambda b, s: (0, b * nts + s)
    tile3 = lambda b, s: (b * nts + s, 0, 0)
    return pl.pallas_call(
        _outproj_router_kernel,
        grid=(B, nts),
        in_specs=[pl.BlockSpec((1, ts, D), lambda b, s: (b, s, 0)),
                  pl.BlockSpec((1, ts, aw), lambda b, s: (b, s, 0)),
                  pl.BlockSpec((1, ts, cw), lambda b, s: (b, s, 0)),
                  pl.BlockSpec((1, aw + cw, D), lambda b, s: (layer, 0, 0)),
                  pl.BlockSpec((None, 1, 6, D), lambda b, s: (layer, b, 0, 0)),
                  pl.BlockSpec((1, D), const),
                  pl.BlockSpec((E, D), const),
                  pl.BlockSpec((E, 1), const),
                  pl.BlockSpec((ts, ts), const),
                  pl.BlockSpec((E, E), const)],
        out_specs=[pl.BlockSpec((1, ts, D), lambda b, s: (b, s, 0)),
                   pl.BlockSpec((ts, D), lambda b, s: (b * nts + s, 0)),
                   pl.BlockSpec((TOP_K, ts), tok),
                   pl.BlockSpec((ts, LANES), lambda b, s: (b * nts + s, 0)),
                   pl.BlockSpec((1, E, LANES), tile3),
                   pl.BlockSpec((1, E, LANES), tile3),
                   pl.BlockSpec((1, E, LANES), tile3),
                   pl.BlockSpec((E, LANES), const)],
        out_shape=[jax.ShapeDtypeStruct((B, S, D), F32),
                   jax.ShapeDtypeStruct((N, D), BF16),
                   jax.ShapeDtypeStruct((TOP_K, N), I32),
                   jax.ShapeDtypeStruct((N, LANES), F32),
                   jax.ShapeDtypeStruct((B * nts, E, LANES), I32),
                   jax.ShapeDtypeStruct((B * nts, E, LANES), I32),
                   jax.ShapeDtypeStruct((B * nts, E, LANES), I32),
                   jax.ShapeDtypeStruct((E, LANES), F32)],
        scratch_shapes=[pltpu.VMEM((E, LANES), F32), pltpu.VMEM((aw + cw, D), BF16)],
        compiler_params=_cparams("arbitrary", "arbitrary"),
        name="outproj_router",
    )(x, attn, conv, w_out, mod, g_ffn.reshape(1, D), w_router.T, b_router.reshape(E, 1), tri, etri)


def _segment_copies(n8_ref, a_ref, b_ref, base, n_exp, copy, wait):
    for e in range(n_exp):
        n = n8_ref[base + e]
        a0 = a_ref[base + e]
        b0 = b_ref[base + e]

        def body(u, c):
            d = copy(pl.multiple_of((a0 + u) * SUB, SUB), pl.multiple_of((b0 + u) * SUB, SUB))
            if wait:
                d.wait()
            else:
                d.start()
            return c

        lax.fori_loop(0, n, body, 0)


def _dispatch_kernel(dst8_ref, n8_ref, src8_ref, tail0_ref, tailn_ref, nu_ref,
                     pos_ref, h_ref, xs_ref, stage_ref, zero_ref, sem, zsem, *, chunk):
    j = pl.program_id(0)
    last = pl.num_programs(0) - 1
    slot = j % 2
    ts = h_ref.shape[0]
    n_exp = tail0_ref.shape[0]
    rows = stage_ref.shape[1]
    h = h_ref[...]
    pos = pos_ref[...]
    for r0 in range(0, rows, chunk):
        rio = lax.broadcasted_iota(I32, (chunk, ts), 0) + r0
        onehot = jnp.zeros((chunk, ts), F32)
        for kk in range(TOP_K):
            onehot = jnp.where(rio == pos[kk:kk + 1, :], 1.0, onehot)
        stage_ref[slot, r0:r0 + chunk, :] = jnp.dot(onehot.astype(BF16), h, preferred_element_type=F32)

    def seg_copy(sl):
        def build(src, dst):
            return pltpu.make_async_copy(stage_ref.at[sl, pl.ds(src, SUB)], xs_ref.at[pl.ds(dst, SUB)], sem)
        return build

    @pl.when(j > 0)
    def _():
        _segment_copies(n8_ref, src8_ref, dst8_ref, (j - 1) * n_exp, n_exp, seg_copy(1 - slot), wait=True)

    _segment_copies(n8_ref, src8_ref, dst8_ref, j * n_exp, n_exp, seg_copy(slot), wait=False)

    @pl.when(j == last)
    def _():
        zero_ref[...] = jnp.zeros_like(zero_ref)

        def zero_copy(u):
            return pltpu.make_async_copy(zero_ref.at[pl.ds(0, SUB)],
                                         xs_ref.at[pl.ds(pl.multiple_of(u * SUB, SUB), SUB)], zsem)

        for e in range(n_exp):
            lo = tail0_ref[e]
            hi = lo + tailn_ref[e]
            lax.fori_loop(lo, hi, lambda u, c: (zero_copy(u).start(), c)[1], 0)
            lax.fori_loop(lo, hi, lambda u, c: (zero_copy(u).wait(), c)[1], 0)

        tm = zero_ref.shape[0]

        def zero_block(b):
            return pltpu.make_async_copy(zero_ref, xs_ref.at[pl.ds(pl.multiple_of(b * tm, tm), tm)], zsem)

        n_blocks = xs_ref.shape[0] // tm
        lax.fori_loop(nu_ref[0], n_blocks, lambda b, c: (zero_block(b).start(), c)[1], 0)
        lax.fori_loop(nu_ref[0], n_blocks, lambda b, c: (zero_block(b).wait(), c)[1], 0)
        _segment_copies(n8_ref, src8_ref, dst8_ref, j * n_exp, n_exp, seg_copy(slot), wait=True)


def _dispatch(dst8, n8, src8, tail0, tailn, n_used, pos, h2, n_rows, ts, tm):
    N, D = h2.shape
    E = tail0.shape[0]
    rows = _stage_rows(ts, E)
    chunk = 256 if rows % 256 == 0 else LANES
    return pl.pallas_call(
        functools.partial(_dispatch_kernel, chunk=chunk),
        grid_spec=pltpu.PrefetchScalarGridSpec(
            num_scalar_prefetch=6,
            grid=(N // ts,),
            in_specs=[pl.BlockSpec((TOP_K, ts), lambda i, *_: (0, i)),
                      pl.BlockSpec((ts, D), lambda i, *_: (i, 0))],
            out_specs=pl.BlockSpec(memory_space=pl.ANY),
            scratch_shapes=[pltpu.VMEM((2, rows, D), F32),
                            pltpu.VMEM((tm, D), F32),
                            pltpu.SemaphoreType.DMA,
                            pltpu.SemaphoreType.DMA]),
        out_shape=jax.ShapeDtypeStruct((n_rows, D), F32),
        compiler_params=_cparams("arbitrary"),
        name="dispatch",
    )(dst8, n8, src8, tail0, tailn, n_used, pos, h2)


def _expert_kernel(be_ref, nu_ref, nx_ref, par_ref, x_ref, b1_ref, b2_ref, w1_hbm, w2_hbm, y_ref,
                   w1f_ref, w2f_ref, w1b_ref, w2b_ref, sem, *, layer):
    i = pl.program_id(0)
    live = i < nu_ref[0]
    e = be_ref[i]
    fresh = jnp.logical_or(i == 0, e != be_ref[jnp.maximum(i - 1, 0)])
    slot = par_ref[i]

    def weight_copies(expert, sl):
        return (pltpu.make_async_copy(w1_hbm.at[layer, expert], w1f_ref.at[sl], sem.at[0, sl]),
                pltpu.make_async_copy(w2_hbm.at[layer, expert], w2f_ref.at[sl], sem.at[1, sl]))

    @pl.when(i == 0)
    def _():
        for cp in weight_copies(e, slot):
            cp.start()

    @pl.when(jnp.logical_and(live, fresh))
    def _():
        for cp in weight_copies(e, slot):
            cp.wait()
        w1b_ref[...] = w1f_ref[slot].astype(BF16)
        w2b_ref[...] = w2f_ref[slot].astype(BF16)
        nxt = nx_ref[i]

        @pl.when(nxt >= 0)
        def _():
            for cp in weight_copies(nxt, 1 - slot):
                cp.start()

    @pl.when(live)
    def _():
        dx = w2b_ref.shape[0]
        hu = jnp.dot(x_ref[...].astype(BF16), w1b_ref[...], preferred_element_type=F32) + b1_ref[0, 0]
        x_glu = jnp.minimum(hu[:, :dx], SWIGLU_LIMIT)
        x_lin = jnp.clip(hu[:, dx:], -SWIGLU_LIMIT, SWIGLU_LIMIT)
        act = (x_lin + 1.0) * (x_glu * jax.nn.sigmoid(SWIGLU_ALPHA * x_glu))
        y_ref[...] = jnp.dot(act.astype(BF16), w2b_ref[...], preferred_element_type=F32) + b2_ref[0, 0]

    @pl.when(jnp.logical_not(live))
    def _():
        y_ref[...] = jnp.zeros_like(y_ref)


def _experts(block_e, n_used, next_e, parity, xs, w1, b1, w2, b2, layer, tm):
    P, D = xs.shape
    _, E, _, H = w1.shape
    dx = w2.shape[2]
    row = lambda i, be, nu, nx, par: (jnp.minimum(i, nu[0] - 1), 0)
    exp4 = lambda i, be, nu, nx, par: (layer, be[i], 0, 0)
    return pl.pallas_call(
        functools.partial(_expert_kernel, layer=layer),
        grid_spec=pltpu.PrefetchScalarGridSpec(
            num_scalar_prefetch=4,
            grid=(P // tm,),
            in_specs=[pl.BlockSpec((tm, D), row),
                      pl.BlockSpec((1, 1, 1, H), exp4),
                      pl.BlockSpec((1, 1, 1, D), exp4),
                      pl.BlockSpec(memory_space=pl.ANY),
                      pl.BlockSpec(memory_space=pl.ANY)],
            out_specs=pl.BlockSpec((tm, D), lambda i, *_: (i, 0)),
            scratch_shapes=[pltpu.VMEM((2, D, H), F32), pltpu.VMEM((2, dx, D), F32),
                            pltpu.VMEM((D, H), BF16), pltpu.VMEM((dx, D), BF16),
                            pltpu.SemaphoreType.DMA((2, 2))]),
        out_shape=jax.ShapeDtypeStruct((P, D), F32),
        compiler_params=_cparams("arbitrary"),
        name="experts",
    )(block_e, n_used, next_e, parity, xs,
      b1.reshape(b1.shape[0], E, 1, H), b2.reshape(b2.shape[0], E, 1, D), w1, w2)


def _combine_kernel(dst8_ref, n8_ref, src8_ref,
                    x_ref, mod_ref, gp_ref, gfin_ref, ys_ref, o_ref, stage_ref, sem, *, final, chunk):
    j = pl.program_id(0) * pl.num_programs(1) + pl.program_id(1)
    n_tiles = pl.num_programs(0) * pl.num_programs(1)
    slot = j % 2
    ts = x_ref.shape[1]
    rows = stage_ref.shape[1]
    n_exp = (rows - ts * TOP_K) // SUB

    def fetch(tile, sl, wait):
        def build(src, dst):
            return pltpu.make_async_copy(ys_ref.at[pl.ds(dst, SUB)], stage_ref.at[sl, pl.ds(src, SUB)], sem.at[sl])
        _segment_copies(n8_ref, src8_ref, dst8_ref, tile * n_exp, n_exp, build, wait)

    def start_fetch(tile, sl):
        stage_ref[sl, ts * TOP_K:rows, :] = jnp.zeros((rows - ts * TOP_K, stage_ref.shape[2]), F32)
        fetch(tile, sl, wait=False)

    @pl.when(j == 0)
    def _():
        start_fetch(0, 0)

    @pl.when(j + 1 < n_tiles)
    def _():
        start_fetch(j + 1, 1 - slot)

    gp = gp_ref[...]
    gates = [jnp.broadcast_to(gp[:, kk:kk + 1], (ts, chunk)) for kk in range(TOP_K)]
    slots = [jnp.broadcast_to(gp[:, TOP_K + kk:TOP_K + kk + 1].astype(I32), (ts, chunk)) for kk in range(TOP_K)]
    fetch(j, slot, wait=True)

    moe = jnp.zeros((ts, stage_ref.shape[2]), F32)
    for r0 in range(0, rows, chunk):
        rio = lax.broadcasted_iota(I32, (ts, chunk), 1) + r0
        wgt = jnp.zeros((ts, chunk), F32)
        for kk in range(TOP_K):
            wgt = jnp.where(rio == slots[kk], gates[kk], wgt)
        moe = moe + jnp.dot(wgt.astype(BF16), stage_ref[slot, r0:r0 + chunk, :].astype(BF16),
                            preferred_element_type=F32)
    gt2 = mod_ref[0][5:6]
    out = x_ref[0] + gt2 * moe
    if final:
        ms = jnp.mean(out * out, axis=-1, keepdims=True)
        out = out * lax.rsqrt(ms + EPS) * gfin_ref[...]
    o_ref[0] = out


def _combine(dst8, n8, src8, x1, mod, layer, gp, g_final, ys, n_exp, ts, final):
    B, S, D = x1.shape
    nts = S // ts
    rows = _stage_rows(ts, n_exp)
    chunk = 256 if rows % 256 == 0 else LANES
    return pl.pallas_call(
        functools.partial(_combine_kernel, final=final, chunk=chunk),
        grid_spec=pltpu.PrefetchScalarGridSpec(
            num_scalar_prefetch=3,
            grid=(B, nts),
            in_specs=[pl.BlockSpec((1, ts, D), lambda b, s, *_: (b, s, 0)),
                      pl.BlockSpec((None, 1, 6, D), lambda b, s, *_: (layer, b, 0, 0)),
                      pl.BlockSpec((ts, LANES), lambda b, s, *_: (b * nts + s, 0)),
                      pl.BlockSpec((1, D), lambda b, s, *_: (0, 0)),
                      pl.BlockSpec(memory_space=pl.ANY)],
            out_specs=pl.BlockSpec((1, ts, D), lambda b, s, *_: (b, s, 0)),
            scratch_shapes=[pltpu.VMEM((2, rows, D), F32), pltpu.SemaphoreType.DMA((2,))]),
        out_shape=jax.ShapeDtypeStruct((B, S, D), F32),
        compiler_params=_cparams("arbitrary", "arbitrary"),
        name="combine",
    )(dst8, n8, src8, x1, mod, gp, g_final.reshape(1, D), ys)


def _rope_tables(seq_len):
    freqs = ROPE_THETA ** (-jnp.arange(ROPE_FREQS, dtype=F32) / ROPE_FREQS)
    pos = jnp.arange(seq_len, dtype=I32)
    row_ang = (pos // GRID_W).astype(F32)[:, None] * freqs
    col_ang = (pos % GRID_W).astype(F32)[:, None] * freqs
    cos = jnp.concatenate([jnp.cos(row_ang)] * 2 + [jnp.cos(col_ang)] * 2, axis=1)
    sin = jnp.concatenate([-jnp.sin(row_ang), jnp.sin(row_ang), -jnp.sin(col_ang), jnp.sin(col_ang)], axis=1)
    reps = LANES // HEAD_DIM
    return jnp.tile(cos, (1, reps)), jnp.tile(sin, (1, reps))


def _tile_rows(n, prefs):
    for t in prefs:
        if n % t == 0:
            return t
    return n


def kernel(x, c, w_mod, b_mod, g_mix, w_in, g_q, g_k, w_dw, b_dw, g_cn, b_cn,
           w_out, g_ffn, w_router, b_router, w1, b1, w2, b2, g_final):
    B, S, D = x.shape
    L = w_mod.shape[0]
    E = w_router.shape[2]
    N = B * S
    ts = _tile_rows(S, (512, 256, 128))
    tq = _tile_rows(S, (256, 128))
    tm = 512
    n_tiles = N // ts
    max_rows = N * TOP_K + n_tiles * E * (SUB - 1)
    n_rows = (max_rows + tm - 1) // tm * tm + E * tm
    n_blocks = n_rows // tm

    mod = _modulation(c, w_mod, b_mod).reshape(L, B, 6, D)
    cos_t, sin_t = _rope_tables(S)
    head = jnp.arange(ATTN_WIDTH, dtype=I32) // HEAD_DIM
    bd = jnp.where(head[:, None] == head[None, :], 1.0 / HEAD_DIM, 0.0).astype(BF16)
    tok = jnp.arange(ts, dtype=I32)
    tri = (tok[:, None] < tok[None, :]).astype(BF16)
    eid = jnp.arange(E, dtype=I32)
    etri = (eid[None, :] < eid[:, None]).astype(BF16)

    for l in range(L):
        q, k, v, u = _inproj(x, mod, l, g_mix[l], w_in, g_q[l], g_k[l], bd, cos_t, sin_t, ts)
        attn = _attention(q, k, v, tq)
        conv = _conv(u, w_dw[l], b_dw[l], g_cn[l], b_cn[l])
        x1, h2, pos, gp, a8, n8, b8, tot = _outproj_router(
            x, attn, conv, w_out, mod, l, g_ffn[l], w_router[l], b_router[l], tri, etri, ts)

        rows_e = tot[:, 0].astype(I32) * SUB
        padded = (rows_e + tm - 1) // tm * tm
        pend = jnp.cumsum(padded)
        pstart = pend - padded
        n_used = (pend[-1] // tm).astype(I32)
        blk = jnp.minimum(jnp.arange(n_blocks, dtype=I32), n_used - 1) * tm
        block_e = jnp.minimum(jnp.sum(blk[:, None] >= pend[None, :], axis=1), E - 1).astype(I32)
        dst8 = (a8[:, :, 0] + (pstart // SUB)[None, :]).reshape(-1)
        n8f = n8[:, :, 0].reshape(-1)
        src8 = b8[:, :, 0].reshape(-1)
        tail0 = (pstart + rows_e) // SUB
        tailn = (padded - rows_e) // SUB

        has = padded > 0
        ordinal = jnp.cumsum(has.astype(I32)) - 1
        later = jnp.where(jnp.logical_and(has[None, :], eid[None, :] > eid[:, None]), eid[None, :], E)
        nxt = jnp.min(later, axis=1)
        next_e = jnp.where(nxt < E, nxt, -1).astype(I32)[block_e]
        parity = (ordinal % 2).astype(I32)[block_e]

        n_used = n_used.reshape(1)
        xs = _dispatch(dst8, n8f, src8, tail0, tailn, n_used, pos, h2, n_rows, ts, tm)
        ys = _experts(block_e, n_used, next_e, parity, xs, w1, b1, w2, b2, l, tm)
        x = _combine(dst8, n8f, src8, x1, mod, l, gp, g_final, ys, E, ts, final=(l == L - 1))
    return x
```

```python
import functools

import jax
import jax.numpy as jnp
from jax import lax
from jax.experimental import pallas as pl
from jax.experimental.pallas import tpu as pltpu

F32 = jnp.float32
BF16 = jnp.bfloat16
I32 = jnp.int32

HEAD_DIM = 64
N_Q_HEADS = 8
N_KV_HEADS = 2
ATTN_WIDTH = N_Q_HEADS * HEAD_DIM
KV_WIDTH = N_KV_HEADS * HEAD_DIM
GRID_W = 64
ROPE_THETA = 10000.0
ROPE_FREQS = HEAD_DIM // 4
CONV_WIDTH = 31
CONV_PAD = CONV_WIDTH // 2
TOP_K = 4
SWIGLU_LIMIT = 7.0
SWIGLU_ALPHA = 1.702
EPS = 1e-6
LOG2E = 1.4426950408889634

LANES = 128
SUB = 8
VMEM_LIMIT = 56 * 1024 * 1024


def _cparams(*sem):
    return pltpu.CompilerParams(dimension_semantics=sem, vmem_limit_bytes=VMEM_LIMIT)


def _stage_rows(ts, n_exp):
    return ts * TOP_K + n_exp * SUB


def _mod_kernel(c_ref, w_ref, b_ref, o_ref):
    c = c_ref[...]
    c_act = (c * jax.nn.sigmoid(c)).astype(BF16)
    o_ref[0] = jnp.dot(c_act, w_ref[0].astype(BF16), preferred_element_type=F32) + b_ref[0]


def _modulation(c, w_mod, b_mod):
    L, D, W = w_mod.shape
    B = c.shape[0]
    tn = 1536
    return pl.pallas_call(
        _mod_kernel,
        grid=(L, W // tn),
        in_specs=[pl.BlockSpec((B, D), lambda l, j: (0, 0)),
                  pl.BlockSpec((1, D, tn), lambda l, j: (l, 0, j)),
                  pl.BlockSpec((1, 1, tn), lambda l, j: (l, 0, j))],
        out_specs=pl.BlockSpec((1, B, tn), lambda l, j: (l, 0, j)),
        out_shape=jax.ShapeDtypeStruct((L, B, W), F32),
        compiler_params=_cparams("parallel", "parallel"),
        name="modulation",
    )(c, w_mod, b_mod.reshape(L, 1, W))


def _head_rmsnorm(t, gain, bd):
    msq = jnp.dot((t * t).astype(BF16), bd, preferred_element_type=F32)
    return t * lax.rsqrt(msq + EPS) * gain


def _rope(t, cos, sin):
    w = t.shape[1]
    reps = w // LANES
    cosw = jnp.concatenate([cos] * reps, axis=1) if reps > 1 else cos
    sinw = jnp.concatenate([sin] * reps, axis=1) if reps > 1 else sin
    lane = lax.broadcasted_iota(I32, t.shape, 1)
    first = (lane & ROPE_FREQS) == 0
    partner = jnp.where(first, pltpu.roll(t, w - ROPE_FREQS, 1), pltpu.roll(t, ROPE_FREQS, 1))
    return t * cosw + partner * sinw


def _inproj_kernel(x_ref, mod_ref, gmix_ref, w_ref, gq_ref, gk_ref, bd_ref, cos_ref, sin_ref,
                   q_ref, k_ref, v_ref, u_ref, wb_ref):
    @pl.when(jnp.logical_and(pl.program_id(0) == 0, pl.program_id(1) == 0))
    def _():
        wb_ref[...] = w_ref[0].astype(BF16)

    x = x_ref[0]
    mod = mod_ref[0]
    sh1, sc1 = mod[0:1], mod[1:2]
    ms = jnp.mean(x * x, axis=-1, keepdims=True)
    xn = x * lax.rsqrt(ms + EPS) * gmix_ref[...]
    h = (xn * (1.0 + sc1) + sh1).astype(BF16)
    proj = jnp.dot(h, wb_ref[...], preferred_element_type=F32)
    a0, a1, a2 = ATTN_WIDTH, ATTN_WIDTH + KV_WIDTH, ATTN_WIDTH + 2 * KV_WIDTH
    cw = (proj.shape[1] - a2) // 2
    q, k, v = proj[:, :a0], proj[:, a0:a1], proj[:, a1:a2]
    ca, cg = proj[:, a2:a2 + cw], proj[:, a2 + cw:]
    cos, sin = cos_ref[...], sin_ref[...]
    bd = bd_ref[...]
    q = _rope(_head_rmsnorm(q, gq_ref[...], bd), cos, sin)
    k = _rope(_head_rmsnorm(k, gk_ref[...], bd[:KV_WIDTH, :KV_WIDTH]), cos, sin)
    q_ref[0] = (q * (HEAD_DIM ** -0.5 * LOG2E)).astype(BF16)
    k_ref[0] = k.astype(BF16)
    v_ref[0] = v.astype(BF16)
    u_ref[0] = ca * jax.nn.sigmoid(cg)


def _inproj(x, mod, layer, g_mix, w_in, g_q, g_k, bd, cos_t, sin_t, ts):
    B, S, D = x.shape
    W = w_in.shape[2]
    cw = (W - ATTN_WIDTH - 2 * KV_WIDTH) // 2
    gq = jnp.tile(g_q, N_Q_HEADS).reshape(1, ATTN_WIDTH)
    gk = jnp.tile(g_k, N_KV_HEADS).reshape(1, KV_WIDTH)
    const = lambda b, s: (0, 0)
    return pl.pallas_call(
        _inproj_kernel,
        grid=(B, S // ts),
        in_specs=[pl.BlockSpec((1, ts, D), lambda b, s: (b, s, 0)),
                  pl.BlockSpec((None, 1, 6, D), lambda b, s: (layer, b, 0, 0)),
                  pl.BlockSpec((1, D), const),
                  pl.BlockSpec((1, D, W), lambda b, s: (layer, 0, 0)),
                  pl.BlockSpec((1, ATTN_WIDTH), const),
                  pl.BlockSpec((1, KV_WIDTH), const),
                  pl.BlockSpec((ATTN_WIDTH, ATTN_WIDTH), const),
                  pl.BlockSpec((ts, LANES), lambda b, s: (s, 0)),
                  pl.BlockSpec((ts, LANES), lambda b, s: (s, 0))],
        out_specs=[pl.BlockSpec((1, ts, ATTN_WIDTH), lambda b, s: (b, s, 0)),
                   pl.BlockSpec((1, ts, KV_WIDTH), lambda b, s: (b, s, 0)),
                   pl.BlockSpec((1, ts, KV_WIDTH), lambda b, s: (b, s, 0)),
                   pl.BlockSpec((1, ts, cw), lambda b, s: (b, s, 0))],
        out_shape=[jax.ShapeDtypeStruct((B, S, ATTN_WIDTH), BF16),
                   jax.ShapeDtypeStruct((B, S, KV_WIDTH), BF16),
                   jax.ShapeDtypeStruct((B, S, KV_WIDTH), BF16),
                   jax.ShapeDtypeStruct((B, S, cw), F32)],
        scratch_shapes=[pltpu.VMEM((D, W), BF16)],
        compiler_params=_cparams("arbitrary", "arbitrary"),
        name="inproj",
    )(x, mod, g_mix.reshape(1, D), w_in, gq, gk, bd, cos_t, sin_t)


def _attn_kernel(q_ref, k_ref, v_ref, o_ref):
    k = k_ref[0]
    v = v_ref[0]
    tq = q_ref.shape[1]
    lane = lax.broadcasted_iota(I32, (tq, LANES), 1)
    low = lane < HEAD_DIM
    heads_per_tile = LANES // HEAD_DIM
    group = N_Q_HEADS // N_KV_HEADS

    def scores(h):
        j, hh = divmod(h, heads_per_tile)
        g = h // group
        qt = q_ref[0, :, j * LANES:(j + 1) * LANES].astype(F32)
        qm = jnp.where(low if hh == 0 else jnp.logical_not(low), qt, 0.0)
        if hh != g:
            qm = pltpu.roll(qm, HEAD_DIM, 1)
        return lax.dot_general(qm.astype(BF16), k, (((1,), (1,)), ((), ())),
                               preferred_element_type=F32)

    s_next = scores(0)
    placed = []
    for h in range(N_Q_HEADS):
        s = s_next
        if h + 1 < N_Q_HEADS:
            s_next = scores(h + 1)
        j, hh = divmod(h, heads_per_tile)
        m = jnp.max(s, axis=-1, keepdims=True)
        p = jnp.exp2(s - m)
        l = jnp.sum(p, axis=-1, keepdims=True)
        o = jnp.dot(p.astype(BF16), v, preferred_element_type=F32) / l
        if hh != h // group:
            o = pltpu.roll(o, HEAD_DIM, 1)
        placed.append(o)
        if hh == heads_per_tile - 1:
            o_ref[0, :, j * LANES:(j + 1) * LANES] = jnp.where(low, placed[0], placed[1]).astype(BF16)
            placed = []


def _attention(q, k, v, tq):
    B, S, _ = q.shape
    return pl.pallas_call(
        _attn_kernel,
        grid=(B, S // tq),
        in_specs=[pl.BlockSpec((1, tq, ATTN_WIDTH), lambda b, i: (b, i, 0)),
                  pl.BlockSpec((1, S, KV_WIDTH), lambda b, i: (b, 0, 0)),
                  pl.BlockSpec((1, S, KV_WIDTH), lambda b, i: (b, 0, 0))],
        out_specs=pl.BlockSpec((1, tq, ATTN_WIDTH), lambda b, i: (b, i, 0)),
        out_shape=jax.ShapeDtypeStruct((B, S, ATTN_WIDTH), BF16),
        compiler_params=_cparams("parallel", "parallel"),
        name="attention",
    )(q, k, v)


def _conv_kernel(u_ref, w_ref, b_ref, g_ref, beta_ref, o_ref, pad_ref, *, tc):
    S, C = u_ref.shape[1], u_ref.shape[2]
    halo = 2 * SUB
    pad_ref[0:halo, :] = jnp.zeros((halo, C), F32)
    pad_ref[halo + S:halo + S + halo, :] = jnp.zeros((halo, C), F32)
    pad_ref[halo:halo + S, :] = u_ref[0]
    span = tc + 2 * halo

    def tile(i, carry):
        s0 = pl.multiple_of(i * tc, tc)
        cols = []
        for c0 in range(0, C, LANES):
            win = pad_ref[pl.ds(s0, span), c0:c0 + LANES]
            acc = jnp.zeros((tc, LANES), F32)
            for r in range(SUB):
                shifted = win if r == 0 else pltpu.roll(win, span - r, 0)
                for j in range(CONV_WIDTH):
                    off = halo - CONV_PAD + j
                    if off % SUB == r:
                        a = off - r
                        acc = acc + shifted[a:a + tc, :] * w_ref[j:j + 1, c0:c0 + LANES]
            cols.append(acc)
        acc = jnp.concatenate(cols, axis=1) + b_ref[...]
        mu = jnp.mean(acc, axis=-1, keepdims=True)
        d = acc - mu
        var = jnp.mean(d * d, axis=-1, keepdims=True)
        y = d * lax.rsqrt(var + EPS) * g_ref[...] + beta_ref[...]
        o_ref[0, pl.ds(s0, tc), :] = (y * jax.nn.sigmoid(y)).astype(BF16)
        return carry

    lax.fori_loop(0, S // tc, tile, 0)


def _conv(u, w_dw, b_dw, g_cn, b_cn, tc=64):
    B, S, C = u.shape
    const = lambda b: (0, 0)
    return pl.pallas_call(
        functools.partial(_conv_kernel, tc=tc),
        grid=(B,),
        in_specs=[pl.BlockSpec((1, S, C), lambda b: (b, 0, 0)),
                  pl.BlockSpec((CONV_WIDTH, C), const),
                  pl.BlockSpec((1, C), const),
                  pl.BlockSpec((1, C), const),
                  pl.BlockSpec((1, C), const)],
        out_specs=pl.BlockSpec((1, S, C), lambda b: (b, 0, 0)),
        out_shape=jax.ShapeDtypeStruct((B, S, C), BF16),
        scratch_shapes=[pltpu.VMEM((S + 4 * SUB, C), F32)],
        compiler_params=_cparams("parallel"),
        name="conv",
    )(u, w_dw, b_dw.reshape(1, C), g_cn.reshape(1, C), b_cn.reshape(1, C))


def _split_bf16(a):
    hi = a.astype(BF16)
    lo = (a - hi.astype(F32)).astype(BF16)
    return hi, lo


def _outproj_router_kernel(x_ref, attn_ref, conv_ref, wout_ref, mod_ref, gffn_ref, wr_ref, br_ref, tri_ref, etri_ref,
                           x1_ref, h2_ref, pos_ref, gp_ref, a8_ref, n8_ref, b8_ref, tot_ref, carry_ref, wb_ref):
    first_step = jnp.logical_and(pl.program_id(0) == 0, pl.program_id(1) == 0)

    @pl.when(first_step)
    def _():
        carry_ref[...] = jnp.zeros_like(carry_ref)
        wb_ref[...] = wout_ref[0].astype(BF16)

    mod = mod_ref[0]
    gt1, sh2, sc2 = mod[2:3], mod[3:4], mod[4:5]
    aw = attn_ref.shape[2]
    mix = (jnp.dot(attn_ref[0], wb_ref[:aw, :], preferred_element_type=F32)
           + jnp.dot(conv_ref[0], wb_ref[aw:, :], preferred_element_type=F32))
    x1 = x_ref[0] + gt1 * mix
    x1_ref[0] = x1
    ms = jnp.mean(x1 * x1, axis=-1, keepdims=True)
    h2 = x1 * lax.rsqrt(ms + EPS) * gffn_ref[...] * (1.0 + sc2) + sh2
    h2_ref[...] = h2.astype(BF16)

    h_hi, h_lo = _split_bf16(h2)
    w_hi, w_lo = _split_bf16(wr_ref[...])
    nt = (((1,), (1,)), ((), ()))
    logits = (lax.dot_general(w_hi, h_hi, nt, preferred_element_type=F32)
              + lax.dot_general(w_lo, h_hi, nt, preferred_element_type=F32)
              + lax.dot_general(w_hi, h_lo, nt, preferred_element_type=F32)) + br_ref[...]
    n_exp, ts = logits.shape
    reps = ts // LANES
    eio = lax.broadcasted_iota(I32, logits.shape, 0).astype(F32)
    vals, hits = [], []
    work = logits
    for _ in range(TOP_K):
        m = jnp.max(work, axis=0, keepdims=True)
        ik = jnp.min(jnp.where(work == m, eio, float(n_exp)), axis=0, keepdims=True)
        hit = eio == ik
        work = jnp.where(hit, -jnp.inf, work)
        vals.append(m)
        hits.append(hit)
    ex = [jnp.exp(v - vals[0]) for v in vals]
    den = ex[0] + ex[1] + ex[2] + ex[3]

    chosen = jnp.zeros(logits.shape, F32)
    for hit in hits:
        chosen = chosen + hit.astype(F32)
    prefix = jnp.dot(chosen.astype(BF16), tri_ref[...], preferred_element_type=F32)
    cnt = jnp.sum(chosen, axis=1, keepdims=True)
    n8 = jnp.broadcast_to(jnp.floor((cnt + (SUB - 1.0)) * (1.0 / SUB)), (n_exp, LANES))
    b8 = jnp.dot(etri_ref[...], n8.astype(BF16), preferred_element_type=F32)
    slot = jnp.concatenate([b8 * float(SUB)] * reps, axis=1) + prefix
    pos = [jnp.sum(jnp.where(hit, slot, 0.0), axis=0, keepdims=True) for hit in hits]
    for kk in range(TOP_K):
        pos_ref[kk:kk + 1, :] = pos[kk].astype(I32)
    a8_ref[0] = carry_ref[...].astype(I32)
    n8_ref[0] = n8.astype(I32)
    b8_ref[0] = b8.astype(I32)
    carry_ref[...] = carry_ref[...] + n8
    tot_ref[...] = carry_ref[...]

    rows = [e / den for e in ex] + pos + [jnp.zeros((LANES - 2 * TOP_K, ts), F32)]
    gp_ref[...] = jnp.concatenate(rows, axis=0).T


def _outproj_router(x, attn, conv, w_out, mod, layer, g_ffn, w_router, b_router, tri, etri, ts):
    B, S, D = x.shape
    N = B * S
    E = w_router.shape[1]
    nts = S // ts
    aw, cw = attn.shape[2], conv.shape[2]
    const = lambda b, s: (0, 0)
    tok = lambda b, s: (0, b * nts + s)
    tile3 = lambda b, s: (b * nts + s, 0, 0)
    return pl.pallas_call(
        _outproj_router_kernel,
        grid=(B, nts),
        in_specs=[pl.BlockSpec((1, ts, D), lambda b, s: (b, s, 0)),
                  pl.BlockSpec((1, ts, aw), lambda b, s: (b, s, 0)),
                  pl.BlockSpec((1, ts, cw), lambda b, s: (b, s, 0)),
                  pl.BlockSpec((1, aw + cw, D), lambda b, s: (layer, 0, 0)),
                  pl.BlockSpec((None, 1, 6, D), lambda b, s: (layer, b, 0, 0)),
                  pl.BlockSpec((1, D), const),
                  pl.BlockSpec((E, D), const),
                  pl.BlockSpec((E, 1), const),
                  pl.BlockSpec((ts, ts), const),
                  pl.BlockSpec((E, E), const)],
        out_specs=[pl.BlockSpec((1, ts, D), lambda b, s: (b, s, 0)),
                   pl.BlockSpec((ts, D), lambda b, s: (b * nts + s, 0)),
                   pl.BlockSpec((TOP_K, ts), tok),
                   pl.BlockSpec((ts, LANES), lambda b, s: (b * nts + s, 0)),
                   pl.BlockSpec((1, E, LANES), tile3),
                   pl.BlockSpec((1, E, LANES), tile3),
                   pl.BlockSpec((1, E, LANES), tile3),
                   pl.BlockSpec((E, LANES), const)],
        out_shape=[jax.ShapeDtypeStruct((B, S, D), F32),
                   jax.ShapeDtypeStruct((N, D), BF16),
                   jax.ShapeDtypeStruct((TOP_K, N), I32),
                   jax.ShapeDtypeStruct((N, LANES), F32),
                   jax.ShapeDtypeStruct((B * nts, E, LANES), I32),
                   jax.ShapeDtypeStruct((B * nts, E, LANES), I32),
                   jax.ShapeDtypeStruct((B * nts, E, LANES), I32),
                   jax.ShapeDtypeStruct((E, LANES), F32)],
        scratch_shapes=[pltpu.VMEM((E, LANES), F32), pltpu.VMEM((aw + cw, D), BF16)],
        compiler_params=_cparams("arbitrary", "arbitrary"),
        name="outproj_router",
    )(x, attn, conv, w_out, mod, g_ffn.reshape(1, D), w_router.T, b_router.reshape(E, 1), tri, etri)


def _start_segment_copies(n8_ref, a_ref, b_ref, base, n_exp, max_units, copy):
    for e in range(n_exp):
        n = n8_ref[base + e]
        a0 = a_ref[base + e]
        b0 = b_ref[base + e]
        for bit in reversed(range(max_units.bit_length())):
            done = (n >> (bit + 1)) << (bit + 1)

            @pl.when(((n >> bit) & 1) == 1)
            def _():
                copy(pl.multiple_of((a0 + done) * SUB, SUB), pl.multiple_of((b0 + done) * SUB, SUB),
                     SUB << bit).start()


def _wait_segment_copies(units, big_units, copy):
    shift = big_units.bit_length() - 1
    lax.fori_loop(0, units >> shift, lambda u, c: (copy(0, 0, SUB * big_units).wait(), c)[1], 0)
    lax.fori_loop(0, units & (big_units - 1), lambda u, c: (copy(0, 0, SUB).wait(), c)[1], 0)


def _dispatch_kernel(dst8_ref, n8_ref, src8_ref, units_ref, tail0_ref, tailn_ref, nu_ref,
                     pos_ref, h_ref, xs_ref, stage_ref, zero_ref, sem, zsem, *, chunk):
    j = pl.program_id(0)
    last = pl.num_programs(0) - 1
    slot = j % 2
    ts = h_ref.shape[0]
    n_exp = tail0_ref.shape[0]
    rows = stage_ref.shape[1]
    h = h_ref[...]
    pos = pos_ref[...]
    for r0 in range(0, rows, chunk):
        rio = lax.broadcasted_iota(I32, (chunk, ts), 0) + r0
        onehot = jnp.zeros((chunk, ts), F32)
        for kk in range(TOP_K):
            onehot = jnp.where(rio == pos[kk:kk + 1, :], 1.0, onehot)
        stage_ref[slot, r0:r0 + chunk, :] = jnp.dot(onehot.astype(BF16), h, preferred_element_type=F32)

    def seg_copy(sl):
        def build(src, dst, nrows):
            return pltpu.make_async_copy(stage_ref.at[sl, pl.ds(src, nrows)], xs_ref.at[pl.ds(dst, nrows)],
                                         sem.at[sl])
        return build

    _start_segment_copies(n8_ref, src8_ref, dst8_ref, j * n_exp, n_exp, ts // SUB, seg_copy(slot))

    @pl.when(j > 0)
    def _():
        _wait_segment_copies(units_ref[j - 1], ts // SUB, seg_copy(1 - slot))

    @pl.when(j == last)
    def _():
        zero_ref[...] = jnp.zeros_like(zero_ref)

        def zero_copy(u):
            return pltpu.make_async_copy(zero_ref.at[pl.ds(0, SUB)],
                                         xs_ref.at[pl.ds(pl.multiple_of(u * SUB, SUB), SUB)], zsem)

        for e in range(n_exp):
            lo = tail0_ref[e]
            hi = lo + tailn_ref[e]
            lax.fori_loop(lo, hi, lambda u, c: (zero_copy(u).start(), c)[1], 0)
            lax.fori_loop(lo, hi, lambda u, c: (zero_copy(u).wait(), c)[1], 0)

        tm = zero_ref.shape[0]

        def zero_block(b):
            return pltpu.make_async_copy(zero_ref, xs_ref.at[pl.ds(pl.multiple_of(b * tm, tm), tm)], zsem)

        n_blocks = xs_ref.shape[0] // tm
        lax.fori_loop(nu_ref[0], n_blocks, lambda b, c: (zero_block(b).start(), c)[1], 0)
        lax.fori_loop(nu_ref[0], n_blocks, lambda b, c: (zero_block(b).wait(), c)[1], 0)
        _wait_segment_copies(units_ref[j], ts // SUB, seg_copy(slot))


def _dispatch(dst8, n8, src8, units, tail0, tailn, n_used, pos, h2, n_rows, ts, tm):
    N, D = h2.shape
    E = tail0.shape[0]
    rows = _stage_rows(ts, E)
    chunk = 256 if rows % 256 == 0 else LANES
    return pl.pallas_call(
        functools.partial(_dispatch_kernel, chunk=chunk),
        grid_spec=pltpu.PrefetchScalarGridSpec(
            num_scalar_prefetch=7,
            grid=(N // ts,),
            in_specs=[pl.BlockSpec((TOP_K, ts), lambda i, *_: (0, i)),
                      pl.BlockSpec((ts, D), lambda i, *_: (i, 0))],
            out_specs=pl.BlockSpec(memory_space=pl.ANY),
            scratch_shapes=[pltpu.VMEM((2, rows, D), F32),
                            pltpu.VMEM((tm, D), F32),
                            pltpu.SemaphoreType.DMA((2,)),
                            pltpu.SemaphoreType.DMA]),
        out_shape=jax.ShapeDtypeStruct((n_rows, D), F32),
        compiler_params=_cparams("arbitrary"),
        name="dispatch",
    )(dst8, n8, src8, units, tail0, tailn, n_used, pos, h2)


def _expert_kernel(be_ref, nu_ref, nx_ref, par_ref, x_ref, b1_ref, b2_ref, w1_hbm, w2_hbm, y_ref,
                   w1f_ref, w2f_ref, w1b_ref, w2b_ref, sem, *, layer):
    i = pl.program_id(0)
    live = i < nu_ref[0]
    e = be_ref[i]
    fresh = jnp.logical_or(i == 0, e != be_ref[jnp.maximum(i - 1, 0)])
    slot = par_ref[i]

    def weight_copies(expert, sl):
        return (pltpu.make_async_copy(w1_hbm.at[layer, expert], w1f_ref.at[sl], sem.at[0, sl]),
                pltpu.make_async_copy(w2_hbm.at[layer, expert], w2f_ref.at[sl], sem.at[1, sl]))

    @pl.when(i == 0)
    def _():
        for cp in weight_copies(e, slot):
            cp.start()

    @pl.when(jnp.logical_and(live, fresh))
    def _():
        for cp in weight_copies(e, slot):
            cp.wait()
        w1b_ref[...] = w1f_ref[slot].astype(BF16)
        w2b_ref[...] = w2f_ref[slot].astype(BF16)
        nxt = nx_ref[i]

        @pl.when(nxt >= 0)
        def _():
            for cp in weight_copies(nxt, 1 - slot):
                cp.start()

    @pl.when(live)
    def _():
        dx = w2b_ref.shape[0]
        hu = jnp.dot(x_ref[...].astype(BF16), w1b_ref[...], preferred_element_type=F32) + b1_ref[0, 0]
        x_glu = jnp.minimum(hu[:, :dx], SWIGLU_LIMIT)
        x_lin = jnp.clip(hu[:, dx:], -SWIGLU_LIMIT, SWIGLU_LIMIT)
        act = (x_lin + 1.0) * (x_glu * jax.nn.sigmoid(SWIGLU_ALPHA * x_glu))
        y_ref[...] = jnp.dot(act.astype(BF16), w2b_ref[...], preferred_element_type=F32) + b2_ref[0, 0]

    @pl.when(jnp.logical_not(live))
    def _():
        y_ref[...] = jnp.zeros_like(y_ref)


def _experts(block_e, n_used, next_e, parity, xs, w1, b1, w2, b2, layer, tm):
    P, D = xs.shape
    _, E, _, H = w1.shape
    dx = w2.shape[2]
    row = lambda i, be, nu, nx, par: (jnp.minimum(i, nu[0] - 1), 0)
    exp4 = lambda i, be, nu, nx, par: (layer, be[i], 0, 0)
    return pl.pallas_call(
        functools.partial(_expert_kernel, layer=layer),
        grid_spec=pltpu.PrefetchScalarGridSpec(
            num_scalar_prefetch=4,
            grid=(P // tm,),
            in_specs=[pl.BlockSpec((tm, D), row),
                      pl.BlockSpec((1, 1, 1, H), exp4),
                      pl.BlockSpec((1, 1, 1, D), exp4),
                      pl.BlockSpec(memory_space=pl.ANY),
                      pl.BlockSpec(memory_space=pl.ANY)],
            out_specs=pl.BlockSpec((tm, D), lambda i, *_: (i, 0)),
            scratch_shapes=[pltpu.VMEM((2, D, H), F32), pltpu.VMEM((2, dx, D), F32),
                            pltpu.VMEM((D, H), BF16), pltpu.VMEM((dx, D), BF16),
                            pltpu.SemaphoreType.DMA((2, 2))]),
        out_shape=jax.ShapeDtypeStruct((P, D), F32),
        compiler_params=_cparams("arbitrary"),
        name="experts",
    )(block_e, n_used, next_e, parity, xs,
      b1.reshape(b1.shape[0], E, 1, H), b2.reshape(b2.shape[0], E, 1, D), w1, w2)


def _combine_kernel(dst8_ref, n8_ref, src8_ref, units_ref,
                    x_ref, mod_ref, gp_ref, gfin_ref, ys_ref, o_ref, stage_ref, sem, *, final, chunk):
    j = pl.program_id(0) * pl.num_programs(1) + pl.program_id(1)
    n_tiles = pl.num_programs(0) * pl.num_programs(1)
    slot = j % 2
    ts = x_ref.shape[1]
    rows = stage_ref.shape[1]
    n_exp = (rows - ts * TOP_K) // SUB

    def seg_copy(sl):
        def build(src, dst, nrows):
            return pltpu.make_async_copy(ys_ref.at[pl.ds(dst, nrows)], stage_ref.at[sl, pl.ds(src, nrows)],
                                         sem.at[sl])
        return build

    def start_fetch(tile, sl):
        stage_ref[sl, ts * TOP_K:rows, :] = jnp.zeros((rows - ts * TOP_K, stage_ref.shape[2]), F32)
        _start_segment_copies(n8_ref, src8_ref, dst8_ref, tile * n_exp, n_exp, ts // SUB, seg_copy(sl))

    @pl.when(j == 0)
    def _():
        start_fetch(0, 0)

    @pl.when(j + 1 < n_tiles)
    def _():
        start_fetch(j + 1, 1 - slot)

    gp = gp_ref[...]
    gates = [jnp.broadcast_to(gp[:, kk:kk + 1], (ts, chunk)) for kk in range(TOP_K)]
    slots = [jnp.broadcast_to(gp[:, TOP_K + kk:TOP_K + kk + 1].astype(I32), (ts, chunk)) for kk in range(TOP_K)]
    _wait_segment_copies(units_ref[j], ts // SUB, seg_copy(slot))

    moe = jnp.zeros((ts, stage_ref.shape[2]), F32)
    for r0 in range(0, rows, chunk):
        rio = lax.broadcasted_iota(I32, (ts, chunk), 1) + r0
        wgt = jnp.zeros((ts, chunk), F32)
        for kk in range(TOP_K):
            wgt = jnp.where(rio == slots[kk], gates[kk], wgt)
        moe = moe + jnp.dot(wgt.astype(BF16), stage_ref[slot, r0:r0 + chunk, :].astype(BF16),
                            preferred_element_type=F32)
    gt2 = mod_ref[0][5:6]
    out = x_ref[0] + gt2 * moe
    if final:
        ms = jnp.mean(out * out, axis=-1, keepdims=True)
        out = out * lax.rsqrt(ms + EPS) * gfin_ref[...]
    o_ref[0] = out


def _combine(dst8, n8, src8, units, x1, mod, layer, gp, g_final, ys, n_exp, ts, final):
    B, S, D = x1.shape
    nts = S // ts
    rows = _stage_rows(ts, n_exp)
    chunk = 256 if rows % 256 == 0 else LANES
    return pl.pallas_call(
        functools.partial(_combine_kernel, final=final, chunk=chunk),
        grid_spec=pltpu.PrefetchScalarGridSpec(
            num_scalar_prefetch=4,
            grid=(B, nts),
            in_specs=[pl.BlockSpec((1, ts, D), lambda b, s, *_: (b, s, 0)),
                      pl.BlockSpec((None, 1, 6, D), lambda b, s, *_: (layer, b, 0, 0)),
                      pl.BlockSpec((ts, LANES), lambda b, s, *_: (b * nts + s, 0)),
                      pl.BlockSpec((1, D), lambda b, s, *_: (0, 0)),
                      pl.BlockSpec(memory_space=pl.ANY)],
            out_specs=pl.BlockSpec((1, ts, D), lambda b, s, *_: (b, s, 0)),
            scratch_shapes=[pltpu.VMEM((2, rows, D), F32), pltpu.SemaphoreType.DMA((2,))]),
        out_shape=jax.ShapeDtypeStruct((B, S, D), F32),
        compiler_params=_cparams("arbitrary", "arbitrary"),
        name="combine",
    )(dst8, n8, src8, units, x1, mod, gp, g_final.reshape(1, D), ys)


def _rope_tables(seq_len):
    freqs = ROPE_THETA ** (-jnp.arange(ROPE_FREQS, dtype=F32) / ROPE_FREQS)
    pos = jnp.arange(seq_len, dtype=I32)
    row_ang = (pos // GRID_W).astype(F32)[:, None] * freqs
    col_ang = (pos % GRID_W).astype(F32)[:, None] * freqs
    cos = jnp.concatenate([jnp.cos(row_ang)] * 2 + [jnp.cos(col_ang)] * 2, axis=1)
    sin = jnp.concatenate([-jnp.sin(row_ang), jnp.sin(row_ang), -jnp.sin(col_ang), jnp.sin(col_ang)], axis=1)
    reps = LANES // HEAD_DIM
    return jnp.tile(cos, (1, reps)), jnp.tile(sin, (1, reps))


def _tile_rows(n, prefs):
    for t in prefs:
        if n % t == 0:
            return t
    return n


def kernel(x, c, w_mod, b_mod, g_mix, w_in, g_q, g_k, w_dw, b_dw, g_cn, b_cn,
           w_out, g_ffn, w_router, b_router, w1, b1, w2, b2, g_final):
    B, S, D = x.shape
    L = w_mod.shape[0]
    E = w_router.shape[2]
    N = B * S
    ts = _tile_rows(S, (512, 256, 128))
    tq = _tile_rows(S, (256, 128))
    tm = 512
    n_tiles = N // ts
    max_rows = N * TOP_K + n_tiles * E * (SUB - 1)
    n_rows = (max_rows + tm - 1) // tm * tm + E * tm
    n_blocks = n_rows // tm

    mod = _modulation(c, w_mod, b_mod).reshape(L, B, 6, D)
    cos_t, sin_t = _rope_tables(S)
    head = jnp.arange(ATTN_WIDTH, dtype=I32) // HEAD_DIM
    bd = jnp.where(head[:, None] == head[None, :], 1.0 / HEAD_DIM, 0.0).astype(BF16)
    tok = jnp.arange(ts, dtype=I32)
    tri = (tok[:, None] < tok[None, :]).astype(BF16)
    eid = jnp.arange(E, dtype=I32)
    etri = (eid[None, :] < eid[:, None]).astype(BF16)

    for l in range(L):
        q, k, v, u = _inproj(x, mod, l, g_mix[l], w_in, g_q[l], g_k[l], bd, cos_t, sin_t, ts)
        attn = _attention(q, k, v, tq)
        conv = _conv(u, w_dw[l], b_dw[l], g_cn[l], b_cn[l])
        x1, h2, pos, gp, a8, n8, b8, tot = _outproj_router(
            x, attn, conv, w_out, mod, l, g_ffn[l], w_router[l], b_router[l], tri, etri, ts)

        rows_e = tot[:, 0].astype(I32) * SUB
        padded = (rows_e + tm - 1) // tm * tm
        pend = jnp.cumsum(padded)
        pstart = pend - padded
        n_used = (pend[-1] // tm).astype(I32)
        blk = jnp.minimum(jnp.arange(n_blocks, dtype=I32), n_used - 1) * tm
        block_e = jnp.minimum(jnp.sum(blk[:, None] >= pend[None, :], axis=1), E - 1).astype(I32)
        dst8 = (a8[:, :, 0] + (pstart // SUB)[None, :]).reshape(-1)
        n8f = n8[:, :, 0].reshape(-1)
        src8 = b8[:, :, 0].reshape(-1)
        units = jnp.sum(n8[:, :, 0], axis=1)
        tail0 = (pstart + rows_e) // SUB
        tailn = (padded - rows_e) // SUB

        has = padded > 0
        ordinal = jnp.cumsum(has.astype(I32)) - 1
        later = jnp.where(jnp.logical_and(has[None, :], eid[None, :] > eid[:, None]), eid[None, :], E)
        nxt = jnp.min(later, axis=1)
        next_e = jnp.where(nxt < E, nxt, -1).astype(I32)[block_e]
        parity = (ordinal % 2).astype(I32)[block_e]

        n_used = n_used.reshape(1)
        xs = _dispatch(dst8, n8f, src8, units, tail0, tailn, n_used, pos, h2, n_rows, ts, tm)
        ys = _experts(block_e, n_used, next_e, parity, xs, w1, b1, w2, b2, l, tm)
        x = _combine(dst8, n8f, src8, units, x1, mod, l, gp, g_final, ys, E, ts, final=(l == L - 1))
    return x
```

```python
import functools

import jax
import jax.numpy as jnp
from jax import lax
from jax.experimental import pallas as pl
from jax.experimental.pallas import tpu as pltpu

F32 = jnp.float32
BF16 = jnp.bfloat16
I32 = jnp.int32

HEAD_DIM = 64
N_Q_HEADS = 8
N_KV_HEADS = 2
ATTN_WIDTH = N_Q_HEADS * HEAD_DIM
KV_WIDTH = N_KV_HEADS * HEAD_DIM
GRID_W = 64
ROPE_THETA = 10000.0
ROPE_FREQS = HEAD_DIM // 4
CONV_WIDTH = 31
CONV_PAD = CONV_WIDTH // 2
TOP_K = 4
SWIGLU_LIMIT = 7.0
SWIGLU_ALPHA = 1.702
EPS = 1e-6
LOG2E = 1.4426950408889634

LANES = 128
SUB = 8
VMEM_LIMIT = 56 * 1024 * 1024


def _cparams(*sem):
    return pltpu.CompilerParams(dimension_semantics=sem, vmem_limit_bytes=VMEM_LIMIT)


def _stage_rows(ts, n_exp):
    return ts * TOP_K + n_exp * SUB


def _mod_kernel(c_ref, w_ref, b_ref, o_ref):
    c = c_ref[...]
    c_act = (c * jax.nn.sigmoid(c)).astype(BF16)
    o_ref[0] = jnp.dot(c_act, w_ref[0].astype(BF16), preferred_element_type=F32) + b_ref[0]


def _modulation(c, w_mod, b_mod):
    L, D, W = w_mod.shape
    B = c.shape[0]
    tn = 1536
    return pl.pallas_call(
        _mod_kernel,
        grid=(L, W // tn),
        in_specs=[pl.BlockSpec((B, D), lambda l, j: (0, 0)),
                  pl.BlockSpec((1, D, tn), lambda l, j: (l, 0, j)),
                  pl.BlockSpec((1, 1, tn), lambda l, j: (l, 0, j))],
        out_specs=pl.BlockSpec((1, B, tn), lambda l, j: (l, 0, j)),
        out_shape=jax.ShapeDtypeStruct((L, B, W), F32),
        compiler_params=_cparams("parallel", "parallel"),
        name="modulation",
    )(c, w_mod, b_mod.reshape(L, 1, W))


def _head_rmsnorm(t, gain, bd):
    msq = jnp.dot((t * t).astype(BF16), bd, preferred_element_type=F32)
    return t * lax.rsqrt(msq + EPS) * gain


def _rope(t, cos, sin):
    w = t.shape[1]
    reps = w // LANES
    cosw = jnp.concatenate([cos] * reps, axis=1) if reps > 1 else cos
    sinw = jnp.concatenate([sin] * reps, axis=1) if reps > 1 else sin
    lane = lax.broadcasted_iota(I32, t.shape, 1)
    first = (lane & ROPE_FREQS) == 0
    partner = jnp.where(first, pltpu.roll(t, w - ROPE_FREQS, 1), pltpu.roll(t, ROPE_FREQS, 1))
    return t * cosw + partner * sinw


def _inproj_kernel(x_ref, mod_ref, gmix_ref, w_ref, gq_ref, gk_ref, bd_ref, cos_ref, sin_ref,
                   q_ref, k_ref, v_ref, u_ref, wb_ref):
    @pl.when(jnp.logical_and(pl.program_id(0) == 0, pl.program_id(1) == 0))
    def _():
        wb_ref[...] = w_ref[0].astype(BF16)

    x = x_ref[0]
    mod = mod_ref[0]
    sh1, sc1 = mod[0:1], mod[1:2]
    ms = jnp.mean(x * x, axis=-1, keepdims=True)
    xn = x * lax.rsqrt(ms + EPS) * gmix_ref[...]
    h = (xn * (1.0 + sc1) + sh1).astype(BF16)
    proj = jnp.dot(h, wb_ref[...], preferred_element_type=F32)
    a0, a1, a2 = ATTN_WIDTH, ATTN_WIDTH + KV_WIDTH, ATTN_WIDTH + 2 * KV_WIDTH
    cw = (proj.shape[1] - a2) // 2
    q, k, v = proj[:, :a0], proj[:, a0:a1], proj[:, a1:a2]
    ca, cg = proj[:, a2:a2 + cw], proj[:, a2 + cw:]
    cos, sin = cos_ref[...], sin_ref[...]
    bd = bd_ref[...]
    q = _rope(_head_rmsnorm(q, gq_ref[...], bd), cos, sin)
    k = _rope(_head_rmsnorm(k, gk_ref[...], bd[:KV_WIDTH, :KV_WIDTH]), cos, sin)
    q_ref[0] = (q * (HEAD_DIM ** -0.5 * LOG2E)).astype(BF16)
    k_ref[0] = k.astype(BF16)
    v_ref[0] = v.astype(BF16)
    u_ref[0] = ca * jax.nn.sigmoid(cg)


def _inproj(x, mod, layer, g_mix, w_in, g_q, g_k, bd, cos_t, sin_t, ts):
    B, S, D = x.shape
    W = w_in.shape[2]
    cw = (W - ATTN_WIDTH - 2 * KV_WIDTH) // 2
    gq = jnp.tile(g_q, N_Q_HEADS).reshape(1, ATTN_WIDTH)
    gk = jnp.tile(g_k, N_KV_HEADS).reshape(1, KV_WIDTH)
    const = lambda b, s: (0, 0)
    return pl.pallas_call(
        _inproj_kernel,
        grid=(B, S // ts),
        in_specs=[pl.BlockSpec((1, ts, D), lambda b, s: (b, s, 0)),
                  pl.BlockSpec((None, 1, 6, D), lambda b, s: (layer, b, 0, 0)),
                  pl.BlockSpec((1, D), const),
                  pl.BlockSpec((1, D, W), lambda b, s: (layer, 0, 0)),
                  pl.BlockSpec((1, ATTN_WIDTH), const),
                  pl.BlockSpec((1, KV_WIDTH), const),
                  pl.BlockSpec((ATTN_WIDTH, ATTN_WIDTH), const),
                  pl.BlockSpec((ts, LANES), lambda b, s: (s, 0)),
                  pl.BlockSpec((ts, LANES), lambda b, s: (s, 0))],
        out_specs=[pl.BlockSpec((1, ts, ATTN_WIDTH), lambda b, s: (b, s, 0)),
                   pl.BlockSpec((1, ts, KV_WIDTH), lambda b, s: (b, s, 0)),
                   pl.BlockSpec((1, ts, KV_WIDTH), lambda b, s: (b, s, 0)),
                   pl.BlockSpec((1, ts, cw), lambda b, s: (b, s, 0))],
        out_shape=[jax.ShapeDtypeStruct((B, S, ATTN_WIDTH), BF16),
                   jax.ShapeDtypeStruct((B, S, KV_WIDTH), BF16),
                   jax.ShapeDtypeStruct((B, S, KV_WIDTH), BF16),
                   jax.ShapeDtypeStruct((B, S, cw), F32)],
        scratch_shapes=[pltpu.VMEM((D, W), BF16)],
        compiler_params=_cparams("arbitrary", "arbitrary"),
        name="inproj",
    )(x, mod, g_mix.reshape(1, D), w_in, gq, gk, bd, cos_t, sin_t)


def _attn_kernel(q_ref, k_ref, v_ref, o_ref):
    k = k_ref[0]
    v = v_ref[0]
    tq = q_ref.shape[1]
    lane = lax.broadcasted_iota(I32, (tq, LANES), 1)
    low = lane < HEAD_DIM
    heads_per_tile = LANES // HEAD_DIM
    group = N_Q_HEADS // N_KV_HEADS

    def scores(h):
        j, hh = divmod(h, heads_per_tile)
        g = h // group
        qt = q_ref[0, :, j * LANES:(j + 1) * LANES].astype(F32)
        qm = jnp.where(low if hh == 0 else jnp.logical_not(low), qt, 0.0)
        if hh != g:
            qm = pltpu.roll(qm, HEAD_DIM, 1)
        return lax.dot_general(qm.astype(BF16), k, (((1,), (1,)), ((), ())),
                               preferred_element_type=F32)

    s_next = scores(0)
    placed = []
    for h in range(N_Q_HEADS):
        s = s_next
        if h + 1 < N_Q_HEADS:
            s_next = scores(h + 1)
        j, hh = divmod(h, heads_per_tile)
        m = jnp.max(s, axis=-1, keepdims=True)
        p = jnp.exp2(s - m)
        l = jnp.sum(p, axis=-1, keepdims=True)
        o = jnp.dot(p.astype(BF16), v, preferred_element_type=F32) / l
        if hh != h // group:
            o = pltpu.roll(o, HEAD_DIM, 1)
        placed.append(o)
        if hh == heads_per_tile - 1:
            o_ref[0, :, j * LANES:(j + 1) * LANES] = jnp.where(low, placed[0], placed[1]).astype(BF16)
            placed = []


def _attention(q, k, v, tq):
    B, S, _ = q.shape
    return pl.pallas_call(
        _attn_kernel,
        grid=(B, S // tq),
        in_specs=[pl.BlockSpec((1, tq, ATTN_WIDTH), lambda b, i: (b, i, 0)),
                  pl.BlockSpec((1, S, KV_WIDTH), lambda b, i: (b, 0, 0)),
                  pl.BlockSpec((1, S, KV_WIDTH), lambda b, i: (b, 0, 0))],
        out_specs=pl.BlockSpec((1, tq, ATTN_WIDTH), lambda b, i: (b, i, 0)),
        out_shape=jax.ShapeDtypeStruct((B, S, ATTN_WIDTH), BF16),
        compiler_params=_cparams("parallel", "parallel"),
        name="attention",
    )(q, k, v)


def _conv_kernel(u_ref, w_ref, b_ref, g_ref, beta_ref, o_ref, pad_ref, *, tc):
    S, C = u_ref.shape[1], u_ref.shape[2]
    halo = 2 * SUB
    pad_ref[0:halo, :] = jnp.zeros((halo, C), F32)
    pad_ref[halo + S:halo + S + halo, :] = jnp.zeros((halo, C), F32)
    pad_ref[halo:halo + S, :] = u_ref[0]
    span = tc + 2 * halo

    def tile(i, carry):
        s0 = pl.multiple_of(i * tc, tc)
        cols = []
        for c0 in range(0, C, LANES):
            win = pad_ref[pl.ds(s0, span), c0:c0 + LANES]
            acc = jnp.zeros((tc, LANES), F32)
            for r in range(SUB):
                shifted = win if r == 0 else pltpu.roll(win, span - r, 0)
                for j in range(CONV_WIDTH):
                    off = halo - CONV_PAD + j
                    if off % SUB == r:
                        a = off - r
                        acc = acc + shifted[a:a + tc, :] * w_ref[j:j + 1, c0:c0 + LANES]
            cols.append(acc)
        acc = jnp.concatenate(cols, axis=1) + b_ref[...]
        mu = jnp.mean(acc, axis=-1, keepdims=True)
        d = acc - mu
        var = jnp.mean(d * d, axis=-1, keepdims=True)
        y = d * lax.rsqrt(var + EPS) * g_ref[...] + beta_ref[...]
        o_ref[0, pl.ds(s0, tc), :] = (y * jax.nn.sigmoid(y)).astype(BF16)
        return carry

    lax.fori_loop(0, S // tc, tile, 0)


def _conv(u, w_dw, b_dw, g_cn, b_cn):
    B, S, C = u.shape
    tc = _tile_rows(S, (128, 64))
    const = lambda b: (0, 0)
    return pl.pallas_call(
        functools.partial(_conv_kernel, tc=tc),
        grid=(B,),
        in_specs=[pl.BlockSpec((1, S, C), lambda b: (b, 0, 0)),
                  pl.BlockSpec((CONV_WIDTH, C), const),
                  pl.BlockSpec((1, C), const),
                  pl.BlockSpec((1, C), const),
                  pl.BlockSpec((1, C), const)],
        out_specs=pl.BlockSpec((1, S, C), lambda b: (b, 0, 0)),
        out_shape=jax.ShapeDtypeStruct((B, S, C), BF16),
        scratch_shapes=[pltpu.VMEM((S + 4 * SUB, C), F32)],
        compiler_params=_cparams("parallel"),
        name="conv",
    )(u, w_dw, b_dw.reshape(1, C), g_cn.reshape(1, C), b_cn.reshape(1, C))


def _split_bf16(a):
    hi = a.astype(BF16)
    lo = (a - hi.astype(F32)).astype(BF16)
    return hi, lo


def _outproj_router_kernel(x_ref, attn_ref, conv_ref, wout_ref, mod_ref, gffn_ref, wr_ref, br_ref, tri_ref, etri_ref,
                           x1_ref, h2_ref, pos_ref, gp_ref, a8_ref, n8_ref, b8_ref, tot_ref, carry_ref, wb_ref):
    first_step = jnp.logical_and(pl.program_id(0) == 0, pl.program_id(1) == 0)

    @pl.when(first_step)
    def _():
        carry_ref[...] = jnp.zeros_like(carry_ref)
        wb_ref[...] = wout_ref[0].astype(BF16)

    mod = mod_ref[0]
    gt1, sh2, sc2 = mod[2:3], mod[3:4], mod[4:5]
    aw = attn_ref.shape[2]
    mix = (jnp.dot(attn_ref[0], wb_ref[:aw, :], preferred_element_type=F32)
           + jnp.dot(conv_ref[0], wb_ref[aw:, :], preferred_element_type=F32))
    x1 = x_ref[0] + gt1 * mix
    x1_ref[0] = x1
    ms = jnp.mean(x1 * x1, axis=-1, keepdims=True)
    h2 = x1 * lax.rsqrt(ms + EPS) * gffn_ref[...] * (1.0 + sc2) + sh2
    h2_ref[...] = h2.astype(BF16)

    h_hi, h_lo = _split_bf16(h2)
    w_hi, w_lo = _split_bf16(wr_ref[...])
    nt = (((1,), (1,)), ((), ()))
    logits = (lax.dot_general(w_hi, h_hi, nt, preferred_element_type=F32)
              + lax.dot_general(w_lo, h_hi, nt, preferred_element_type=F32)
              + lax.dot_general(w_hi, h_lo, nt, preferred_element_type=F32)) + br_ref[...]
    n_exp, ts = logits.shape
    reps = ts // LANES
    eio = lax.broadcasted_iota(I32, logits.shape, 0).astype(F32)
    vals, hits = [], []
    work = logits
    for _ in range(TOP_K):
        m = jnp.max(work, axis=0, keepdims=True)
        ik = jnp.min(jnp.where(work == m, eio, float(n_exp)), axis=0, keepdims=True)
        hit = eio == ik
        work = jnp.where(hit, -jnp.inf, work)
        vals.append(m)
        hits.append(hit)
    ex = [jnp.exp(v - vals[0]) for v in vals]
    den = ex[0] + ex[1] + ex[2] + ex[3]

    chosen = jnp.zeros(logits.shape, F32)
    for hit in hits:
        chosen = chosen + hit.astype(F32)
    prefix = jnp.dot(chosen.astype(BF16), tri_ref[...], preferred_element_type=F32)
    cnt = jnp.sum(chosen, axis=1, keepdims=True)
    n8 = jnp.broadcast_to(jnp.floor((cnt + (SUB - 1.0)) * (1.0 / SUB)), (n_exp, LANES))
    b8 = jnp.dot(etri_ref[...], n8.astype(BF16), preferred_element_type=F32)
    slot = jnp.concatenate([b8 * float(SUB)] * reps, axis=1) + prefix
    pos = [jnp.sum(jnp.where(hit, slot, 0.0), axis=0, keepdims=True) for hit in hits]
    for kk in range(TOP_K):
        pos_ref[kk:kk + 1, :] = pos[kk].astype(I32)
    a8_ref[0] = carry_ref[...].astype(I32)
    n8_ref[0] = n8.astype(I32)
    b8_ref[0] = b8.astype(I32)
    carry_ref[...] = carry_ref[...] + n8
    tot_ref[...] = carry_ref[...]

    rows = [e / den for e in ex] + pos + [jnp.zeros((LANES - 2 * TOP_K, ts), F32)]
    gp_ref[...] = jnp.concatenate(rows, axis=0).T


def _outproj_router(x, attn, conv, w_out, mod, layer, g_ffn, w_router, b_router, tri, etri, ts):
    B, S, D = x.shape
    N = B * S
    E = w_router.shape[1]
    nts = S // ts
    aw, cw = attn.shape[2], conv.shape[2]
    const = lambda b, s: (0, 0)
    tok = lambda b, s: (0, b * nts + s)
    tile3 = lambda b, s: (b * nts + s, 0, 0)
    return pl.pallas_call(
        _outproj_router_kernel,
        grid=(B, nts),
        in_specs=[pl.BlockSpec((1, ts, D), lambda b, s: (b, s, 0)),
                  pl.BlockSpec((1, ts, aw), lambda b, s: (b, s, 0)),
                  pl.BlockSpec((1, ts, cw), lambda b, s: (b, s, 0)),
                  pl.BlockSpec((1, aw + cw, D), lambda b, s: (layer, 0, 0)),
                  pl.BlockSpec((None, 1, 6, D), lambda b, s: (layer, b, 0, 0)),
                  pl.BlockSpec((1, D), const),
                  pl.BlockSpec((E, D), const),
                  pl.BlockSpec((E, 1), const),
                  pl.BlockSpec((ts, ts), const),
                  pl.BlockSpec((E, E), const)],
        out_specs=[pl.BlockSpec((1, ts, D), lambda b, s: (b, s, 0)),
                   pl.BlockSpec((ts, D), lambda b, s: (b * nts + s, 0)),
                   pl.BlockSpec((TOP_K, ts), tok),
                   pl.BlockSpec((ts, LANES), lambda b, s: (b * nts + s, 0)),
                   pl.BlockSpec((1, E, LANES), tile3),
                   pl.BlockSpec((1, E, LANES), tile3),
                   pl.BlockSpec((1, E, LANES), tile3),
                   pl.BlockSpec((E, LANES), const)],
        out_shape=[jax.ShapeDtypeStruct((B, S, D), F32),
                   jax.ShapeDtypeStruct((N, D), BF16),
                   jax.ShapeDtypeStruct((TOP_K, N), I32),
                   jax.ShapeDtypeStruct((N, LANES), F32),
                   jax.ShapeDtypeStruct((B * nts, E, LANES), I32),
                   jax.ShapeDtypeStruct((B * nts, E, LANES), I32),
                   jax.ShapeDtypeStruct((B * nts, E, LANES), I32),
                   jax.ShapeDtypeStruct((E, LANES), F32)],
        scratch_shapes=[pltpu.VMEM((E, LANES), F32), pltpu.VMEM((aw + cw, D), BF16)],
        compiler_params=_cparams("arbitrary", "arbitrary"),
        name="outproj_router",
    )(x, attn, conv, w_out, mod, g_ffn.reshape(1, D), w_router.T, b_router.reshape(E, 1), tri, etri)


def _start_segment_copies(n8_ref, a_ref, b_ref, base, n_exp, max_units, copy):
    n_bits = max_units.bit_length()
    common = min(4, n_bits)
    for e in range(n_exp):
        n = n8_ref[base + e]
        a0 = a_ref[base + e]
        b0 = b_ref[base + e]

        def chunk(bit):
            done = (n >> (bit + 1)) << (bit + 1)

            @pl.when(((n >> bit) & 1) == 1)
            def _():
                copy(pl.multiple_of((a0 + done) * SUB, SUB), pl.multiple_of((b0 + done) * SUB, SUB),
                     SUB << bit).start()

        for bit in reversed(range(common)):
            chunk(bit)

        @pl.when(n >= (1 << common))
        def _():
            for bit in reversed(range(common, n_bits)):
                chunk(bit)


def _wait_segment_copies(units, big_units, copy):
    shift = big_units.bit_length() - 1
    lax.fori_loop(0, units >> shift, lambda u, c: (copy(0, 0, SUB * big_units).wait(), c)[1], 0)
    lax.fori_loop(0, units & (big_units - 1), lambda u, c: (copy(0, 0, SUB).wait(), c)[1], 0)


def _dispatch_kernel(dst8_ref, n8_ref, src8_ref, units_ref, tail0_ref, tailn_ref, nu_ref,
                     pos_ref, h_ref, xs_ref, stage_ref, zero_ref, sem, zsem, *, chunk):
    j = pl.program_id(0)
    last = pl.num_programs(0) - 1
    slot = j % 2
    ts = h_ref.shape[0]
    n_exp = tail0_ref.shape[0]
    rows = stage_ref.shape[1]
    h = h_ref[...]
    pos = pos_ref[...]
    for r0 in range(0, rows, chunk):
        rio = lax.broadcasted_iota(I32, (chunk, ts), 0) + r0
        onehot = jnp.zeros((chunk, ts), F32)
        for kk in range(TOP_K):
            onehot = jnp.where(rio == pos[kk:kk + 1, :], 1.0, onehot)
        stage_ref[slot, r0:r0 + chunk, :] = jnp.dot(onehot.astype(BF16), h, preferred_element_type=F32)

    def seg_copy(sl):
        def build(src, dst, nrows):
            return pltpu.make_async_copy(stage_ref.at[sl, pl.ds(src, nrows)], xs_ref.at[pl.ds(dst, nrows)],
                                         sem.at[sl])
        return build

    _start_segment_copies(n8_ref, src8_ref, dst8_ref, j * n_exp, n_exp, ts // SUB, seg_copy(slot))

    @pl.when(j > 0)
    def _():
        _wait_segment_copies(units_ref[j - 1], ts // SUB, seg_copy(1 - slot))

    @pl.when(j == last)
    def _():
        zero_ref[...] = jnp.zeros_like(zero_ref)

        def zero_copy(u):
            return pltpu.make_async_copy(zero_ref.at[pl.ds(0, SUB)],
                                         xs_ref.at[pl.ds(pl.multiple_of(u * SUB, SUB), SUB)], zsem)

        for e in range(n_exp):
            lo = tail0_ref[e]
            hi = lo + tailn_ref[e]
            lax.fori_loop(lo, hi, lambda u, c: (zero_copy(u).start(), c)[1], 0)
            lax.fori_loop(lo, hi, lambda u, c: (zero_copy(u).wait(), c)[1], 0)

        tm = zero_ref.shape[0]

        def zero_block(b):
            return pltpu.make_async_copy(zero_ref, xs_ref.at[pl.ds(pl.multiple_of(b * tm, tm), tm)], zsem)

        n_blocks = xs_ref.shape[0] // tm
        lax.fori_loop(nu_ref[0], n_blocks, lambda b, c: (zero_block(b).start(), c)[1], 0)
        lax.fori_loop(nu_ref[0], n_blocks, lambda b, c: (zero_block(b).wait(), c)[1], 0)
        _wait_segment_copies(units_ref[j], ts // SUB, seg_copy(slot))


def _dispatch(dst8, n8, src8, units, tail0, tailn, n_used, pos, h2, n_rows, ts, tm):
    N, D = h2.shape
    E = tail0.shape[0]
    rows = _stage_rows(ts, E)
    chunk = 256 if rows % 256 == 0 else LANES
    return pl.pallas_call(
        functools.partial(_dispatch_kernel, chunk=chunk),
        grid_spec=pltpu.PrefetchScalarGridSpec(
            num_scalar_prefetch=7,
            grid=(N // ts,),
            in_specs=[pl.BlockSpec((TOP_K, ts), lambda i, *_: (0, i)),
                      pl.BlockSpec((ts, D), lambda i, *_: (i, 0))],
            out_specs=pl.BlockSpec(memory_space=pl.ANY),
            scratch_shapes=[pltpu.VMEM((2, rows, D), F32),
                            pltpu.VMEM((tm, D), F32),
                            pltpu.SemaphoreType.DMA((2,)),
                            pltpu.SemaphoreType.DMA]),
        out_shape=jax.ShapeDtypeStruct((n_rows, D), F32),
        compiler_params=_cparams("arbitrary"),
        name="dispatch",
    )(dst8, n8, src8, units, tail0, tailn, n_used, pos, h2)


def _expert_kernel(be_ref, nu_ref, nx_ref, par_ref, x_ref, b1_ref, b2_ref, w1_hbm, w2_hbm, y_ref,
                   w1f_ref, w2f_ref, w1b_ref, w2b_ref, sem, *, layer):
    i = pl.program_id(0)
    live = i < nu_ref[0]
    e = be_ref[i]
    fresh = jnp.logical_or(i == 0, e != be_ref[jnp.maximum(i - 1, 0)])
    slot = par_ref[i]

    def weight_copies(expert, sl):
        return (pltpu.make_async_copy(w1_hbm.at[layer, expert], w1f_ref.at[sl], sem.at[0, sl]),
                pltpu.make_async_copy(w2_hbm.at[layer, expert], w2f_ref.at[sl], sem.at[1, sl]))

    @pl.when(i == 0)
    def _():
        for cp in weight_copies(e, slot):
            cp.start()

    @pl.when(jnp.logical_and(live, fresh))
    def _():
        for cp in weight_copies(e, slot):
            cp.wait()
        w1b_ref[...] = w1f_ref[slot].astype(BF16)
        w2b_ref[...] = w2f_ref[slot].astype(BF16)
        nxt = nx_ref[i]

        @pl.when(nxt >= 0)
        def _():
            for cp in weight_copies(nxt, 1 - slot):
                cp.start()

    @pl.when(live)
    def _():
        dx = w2b_ref.shape[0]
        hu = jnp.dot(x_ref[...].astype(BF16), w1b_ref[...], preferred_element_type=F32) + b1_ref[0, 0]
        x_glu = jnp.minimum(hu[:, :dx], SWIGLU_LIMIT)
        x_lin = jnp.clip(hu[:, dx:], -SWIGLU_LIMIT, SWIGLU_LIMIT)
        act = (x_lin + 1.0) * (x_glu * jax.nn.sigmoid(SWIGLU_ALPHA * x_glu))
        y_ref[...] = jnp.dot(act.astype(BF16), w2b_ref[...], preferred_element_type=F32) + b2_ref[0, 0]

    @pl.when(jnp.logical_not(live))
    def _():
        y_ref[...] = jnp.zeros_like(y_ref)


def _experts(block_e, n_used, next_e, parity, xs, w1, b1, w2, b2, layer, tm):
    P, D = xs.shape
    _, E, _, H = w1.shape
    dx = w2.shape[2]
    row = lambda i, be, nu, nx, par: (jnp.minimum(i, nu[0] - 1), 0)
    exp4 = lambda i, be, nu, nx, par: (layer, be[i], 0, 0)
    return pl.pallas_call(
        functools.partial(_expert_kernel, layer=layer),
        grid_spec=pltpu.PrefetchScalarGridSpec(
            num_scalar_prefetch=4,
            grid=(P // tm,),
            in_specs=[pl.BlockSpec((tm, D), row),
                      pl.BlockSpec((1, 1, 1, H), exp4),
                      pl.BlockSpec((1, 1, 1, D), exp4),
                      pl.BlockSpec(memory_space=pl.ANY),
                      pl.BlockSpec(memory_space=pl.ANY)],
            out_specs=pl.BlockSpec((tm, D), lambda i, *_: (i, 0)),
            scratch_shapes=[pltpu.VMEM((2, D, H), F32), pltpu.VMEM((2, dx, D), F32),
                            pltpu.VMEM((D, H), BF16), pltpu.VMEM((dx, D), BF16),
                            pltpu.SemaphoreType.DMA((2, 2))]),
        out_shape=jax.ShapeDtypeStruct((P, D), F32),
        compiler_params=_cparams("arbitrary"),
        name="experts",
    )(block_e, n_used, next_e, parity, xs,
      b1.reshape(b1.shape[0], E, 1, H), b2.reshape(b2.shape[0], E, 1, D), w1, w2)


def _combine_kernel(dst8_ref, n8_ref, src8_ref, units_ref,
                    x_ref, mod_ref, gp_ref, gfin_ref, ys_ref, o_ref, stage_ref, sem, *, final, chunk):
    j = pl.program_id(0) * pl.num_programs(1) + pl.program_id(1)
    n_tiles = pl.num_programs(0) * pl.num_programs(1)
    slot = j % 2
    ts = x_ref.shape[1]
    rows = stage_ref.shape[1]
    n_exp = (rows - ts * TOP_K) // SUB

    def seg_copy(sl):
        def build(src, dst, nrows):
            return pltpu.make_async_copy(ys_ref.at[pl.ds(dst, nrows)], stage_ref.at[sl, pl.ds(src, nrows)],
                                         sem.at[sl])
        return build

    def start_fetch(tile, sl):
        stage_ref[sl, ts * TOP_K:rows, :] = jnp.zeros((rows - ts * TOP_K, stage_ref.shape[2]), F32)
        _start_segment_copies(n8_ref, src8_ref, dst8_ref, tile * n_exp, n_exp, ts // SUB, seg_copy(sl))

    @pl.when(j == 0)
    def _():
        start_fetch(0, 0)

    @pl.when(j + 1 < n_tiles)
    def _():
        start_fetch(j + 1, 1 - slot)

    gp = gp_ref[...]
    gates = [jnp.broadcast_to(gp[:, kk:kk + 1], (ts, chunk)) for kk in range(TOP_K)]
    slots = [jnp.broadcast_to(gp[:, TOP_K + kk:TOP_K + kk + 1].astype(I32), (ts, chunk)) for kk in range(TOP_K)]
    _wait_segment_copies(units_ref[j], ts // SUB, seg_copy(slot))

    moe = jnp.zeros((ts, stage_ref.shape[2]), F32)
    for r0 in range(0, rows, chunk):
        rio = lax.broadcasted_iota(I32, (ts, chunk), 1) + r0
        wgt = jnp.zeros((ts, chunk), F32)
        for kk in range(TOP_K):
            wgt = jnp.where(rio == slots[kk], gates[kk], wgt)
        moe = moe + jnp.dot(wgt.astype(BF16), stage_ref[slot, r0:r0 + chunk, :].astype(BF16),
                            preferred_element_type=F32)
    gt2 = mod_ref[0][5:6]
    out = x_ref[0] + gt2 * moe
    if final:
        ms = jnp.mean(out * out, axis=-1, keepdims=True)
        out = out * lax.rsqrt(ms + EPS) * gfin_ref[...]
    o_ref[0] = out


def _combine(dst8, n8, src8, units, x1, mod, layer, gp, g_final, ys, n_exp, ts, final):
    B, S, D = x1.shape
    nts = S // ts
    rows = _stage_rows(ts, n_exp)
    chunk = 256 if rows % 256 == 0 else LANES
    return pl.pallas_call(
        functools.partial(_combine_kernel, final=final, chunk=chunk),
        grid_spec=pltpu.PrefetchScalarGridSpec(
            num_scalar_prefetch=4,
            grid=(B, nts),
            in_specs=[pl.BlockSpec((1, ts, D), lambda b, s, *_: (b, s, 0)),
                      pl.BlockSpec((None, 1, 6, D), lambda b, s, *_: (layer, b, 0, 0)),
                      pl.BlockSpec((ts, LANES), lambda b, s, *_: (b * nts + s, 0)),
                      pl.BlockSpec((1, D), lambda b, s, *_: (0, 0)),
                      pl.BlockSpec(memory_space=pl.ANY)],
            out_specs=pl.BlockSpec((1, ts, D), lambda b, s, *_: (b, s, 0)),
            scratch_shapes=[pltpu.VMEM((2, rows, D), F32), pltpu.SemaphoreType.DMA((2,))]),
        out_shape=jax.ShapeDtypeStruct((B, S, D), F32),
        compiler_params=_cparams("arbitrary", "arbitrary"),
        name="combine",
    )(dst8, n8, src8, units, x1, mod, gp, g_final.reshape(1, D), ys)


def _rope_tables(seq_len):
    freqs = ROPE_THETA ** (-jnp.arange(ROPE_FREQS, dtype=F32) / ROPE_FREQS)
    pos = jnp.arange(seq_len, dtype=I32)
    row_ang = (pos // GRID_W).astype(F32)[:, None] * freqs
    col_ang = (pos % GRID_W).astype(F32)[:, None] * freqs
    cos = jnp.concatenate([jnp.cos(row_ang)] * 2 + [jnp.cos(col_ang)] * 2, axis=1)
    sin = jnp.concatenate([-jnp.sin(row_ang), jnp.sin(row_ang), -jnp.sin(col_ang), jnp.sin(col_ang)], axis=1)
    reps = LANES // HEAD_DIM
    return jnp.tile(cos, (1, reps)), jnp.tile(sin, (1, reps))


def _tile_rows(n, prefs):
    for t in prefs:
        if n % t == 0:
            return t
    return n


def kernel(x, c, w_mod, b_mod, g_mix, w_in, g_q, g_k, w_dw, b_dw, g_cn, b_cn,
           w_out, g_ffn, w_router, b_router, w1, b1, w2, b2, g_final):
    B, S, D = x.shape
    L = w_mod.shape[0]
    E = w_router.shape[2]
    N = B * S
    ts = _tile_rows(S, (512, 256, 128))
    tq = _tile_rows(S, (256, 128))
    tm = 512
    n_tiles = N // ts
    max_rows = N * TOP_K + n_tiles * E * (SUB - 1)
    n_rows = (max_rows + tm - 1) // tm * tm + E * tm
    n_blocks = n_rows // tm

    mod = _modulation(c, w_mod, b_mod).reshape(L, B, 6, D)
    cos_t, sin_t = _rope_tables(S)
    head = jnp.arange(ATTN_WIDTH, dtype=I32) // HEAD_DIM
    bd = jnp.where(head[:, None] == head[None, :], 1.0 / HEAD_DIM, 0.0).astype(BF16)
    tok = jnp.arange(ts, dtype=I32)
    tri = (tok[:, None] < tok[None, :]).astype(BF16)
    eid = jnp.arange(E, dtype=I32)
    etri = (eid[None, :] < eid[:, None]).astype(BF16)

    for l in range(L):
        q, k, v, u = _inproj(x, mod, l, g_mix[l], w_in, g_q[l], g_k[l], bd, cos_t, sin_t, ts)
        attn = _attention(q, k, v, tq)
        conv = _conv(u, w_dw[l], b_dw[l], g_cn[l], b_cn[l])
        x1, h2, pos, gp, a8, n8, b8, tot = _outproj_router(
            x, attn, conv, w_out, mod, l, g_ffn[l], w_router[l], b_router[l], tri, etri, ts)

        rows_e = tot[:, 0].astype(I32) * SUB
        padded = (rows_e + tm - 1) // tm * tm
        pend = jnp.cumsum(padded)
        pstart = pend - padded
        n_used = (pend[-1] // tm).astype(I32)
        blk = jnp.minimum(jnp.arange(n_blocks, dtype=I32), n_used - 1) * tm
        block_e = jnp.minimum(jnp.sum(blk[:, None] >= pend[None, :], axis=1), E - 1).astype(I32)
        dst8 = (a8[:, :, 0] + (pstart // SUB)[None, :]).reshape(-1)
        n8f = n8[:, :, 0].reshape(-1)
        src8 = b8[:, :, 0].reshape(-1)
        units = jnp.sum(n8[:, :, 0], axis=1)
        tail0 = (pstart + rows_e) // SUB
        tailn = (padded - rows_e) // SUB

        has = padded > 0
        ordinal = jnp.cumsum(has.astype(I32)) - 1
        later = jnp.where(jnp.logical_and(has[None, :], eid[None, :] > eid[:, None]), eid[None, :], E)
        nxt = jnp.min(later, axis=1)
        next_e = jnp.where(nxt < E, nxt, -1).astype(I32)[block_e]
        parity = (ordinal % 2).astype(I32)[block_e]

        n_used = n_used.reshape(1)
        xs = _dispatch(dst8, n8f, src8, units, tail0, tailn, n_used, pos, h2, n_rows, ts, tm)
        ys = _experts(block_e, n_used, next_e, parity, xs, w1, b1, w2, b2, l, tm)
        x = _combine(dst8, n8f, src8, units, x1, mod, l, gp, g_final, ys, E, ts, final=(l == L - 1))
    return x
```

```python
import functools

import jax
import jax.numpy as jnp
from jax import lax
from jax.experimental import pallas as pl
from jax.experimental.pallas import tpu as pltpu

F32 = jnp.float32
BF16 = jnp.bfloat16
I32 = jnp.int32
U32 = jnp.uint32

HEAD_DIM = 64
N_Q_HEADS = 8
N_KV_HEADS = 2
ATTN_WIDTH = N_Q_HEADS * HEAD_DIM
KV_WIDTH = N_KV_HEADS * HEAD_DIM
GRID_W = 64
ROPE_THETA = 10000.0
ROPE_FREQS = HEAD_DIM // 4
CONV_WIDTH = 31
CONV_PAD = CONV_WIDTH // 2
TOP_K = 4
SWIGLU_LIMIT = 7.0
SWIGLU_ALPHA = 1.702
EPS = 1e-6
LOG2E = 1.4426950408889634

LANES = 128
SUB = 8
VMEM_LIMIT = 56 * 1024 * 1024


def _cparams(*sem):
    return pltpu.CompilerParams(dimension_semantics=sem, vmem_limit_bytes=VMEM_LIMIT)


def _stage_rows(ts, n_exp):
    return ts * TOP_K + n_exp * SUB


def _pack_bf16_pairs(t):
    w = t.shape[1] // 2
    bits = pltpu.bitcast(t.astype(BF16).astype(F32), U32)
    return (bits[:, :w] >> 16) | (bits[:, w:] & jnp.uint32(0xFFFF0000))


def _unpack_bf16_pairs(words):
    lo = pltpu.bitcast(words << 16, F32).astype(BF16)
    hi = pltpu.bitcast(words & jnp.uint32(0xFFFF0000), F32).astype(BF16)
    return lo, hi


def _mod_kernel(c_ref, w_ref, b_ref, o_ref):
    c = c_ref[...]
    c_act = (c * jax.nn.sigmoid(c)).astype(BF16)
    o_ref[0] = jnp.dot(c_act, w_ref[0].astype(BF16), preferred_element_type=F32) + b_ref[0]


def _modulation(c, w_mod, b_mod):
    L, D, W = w_mod.shape
    B = c.shape[0]
    tn = 1536
    return pl.pallas_call(
        _mod_kernel,
        grid=(L, W // tn),
        in_specs=[pl.BlockSpec((B, D), lambda l, j: (0, 0)),
                  pl.BlockSpec((1, D, tn), lambda l, j: (l, 0, j)),
                  pl.BlockSpec((1, 1, tn), lambda l, j: (l, 0, j))],
        out_specs=pl.BlockSpec((1, B, tn), lambda l, j: (l, 0, j)),
        out_shape=jax.ShapeDtypeStruct((L, B, W), F32),
        compiler_params=_cparams("parallel", "parallel"),
        name="modulation",
    )(c, w_mod, b_mod.reshape(L, 1, W))


def _head_rmsnorm(t, gain, bd):
    msq = jnp.dot((t * t).astype(BF16), bd, preferred_element_type=F32)
    return t * lax.rsqrt(msq + EPS) * gain


def _rope(t, cos, sin):
    w = t.shape[1]
    reps = w // LANES
    cosw = jnp.concatenate([cos] * reps, axis=1) if reps > 1 else cos
    sinw = jnp.concatenate([sin] * reps, axis=1) if reps > 1 else sin
    lane = lax.broadcasted_iota(I32, t.shape, 1)
    first = (lane & ROPE_FREQS) == 0
    partner = jnp.where(first, pltpu.roll(t, w - ROPE_FREQS, 1), pltpu.roll(t, ROPE_FREQS, 1))
    return t * cosw + partner * sinw


def _inproj_kernel(x_ref, mod_ref, gmix_ref, w_ref, gq_ref, gk_ref, bd_ref, cos_ref, sin_ref,
                   q_ref, k_ref, v_ref, u_ref, wb_ref):
    @pl.when(jnp.logical_and(pl.program_id(0) == 0, pl.program_id(1) == 0))
    def _():
        wb_ref[...] = w_ref[0].astype(BF16)

    x = x_ref[0]
    mod = mod_ref[0]
    sh1, sc1 = mod[0:1], mod[1:2]
    ms = jnp.mean(x * x, axis=-1, keepdims=True)
    xn = x * lax.rsqrt(ms + EPS) * gmix_ref[...]
    h = (xn * (1.0 + sc1) + sh1).astype(BF16)
    proj = jnp.dot(h, wb_ref[...], preferred_element_type=F32)
    a0, a1, a2 = ATTN_WIDTH, ATTN_WIDTH + KV_WIDTH, ATTN_WIDTH + 2 * KV_WIDTH
    cw = (proj.shape[1] - a2) // 2
    q, k, v = proj[:, :a0], proj[:, a0:a1], proj[:, a1:a2]
    ca, cg = proj[:, a2:a2 + cw], proj[:, a2 + cw:]
    cos, sin = cos_ref[...], sin_ref[...]
    bd = bd_ref[...]
    q = _rope(_head_rmsnorm(q, gq_ref[...], bd), cos, sin)
    k = _rope(_head_rmsnorm(k, gk_ref[...], bd[:KV_WIDTH, :KV_WIDTH]), cos, sin)
    q_ref[0] = (q * (HEAD_DIM ** -0.5 * LOG2E)).astype(BF16)
    k_ref[0] = k.astype(BF16)
    v_ref[0] = v.astype(BF16)
    u_ref[0] = ca * jax.nn.sigmoid(cg)


def _inproj(x, mod, layer, g_mix, w_in, g_q, g_k, bd, cos_t, sin_t, ts):
    B, S, D = x.shape
    W = w_in.shape[2]
    cw = (W - ATTN_WIDTH - 2 * KV_WIDTH) // 2
    gq = jnp.tile(g_q, N_Q_HEADS).reshape(1, ATTN_WIDTH)
    gk = jnp.tile(g_k, N_KV_HEADS).reshape(1, KV_WIDTH)
    const = lambda b, s: (0, 0)
    return pl.pallas_call(
        _inproj_kernel,
        grid=(B, S // ts),
        in_specs=[pl.BlockSpec((1, ts, D), lambda b, s: (b, s, 0)),
                  pl.BlockSpec((None, 1, 6, D), lambda b, s: (layer, b, 0, 0)),
                  pl.BlockSpec((1, D), const),
                  pl.BlockSpec((1, D, W), lambda b, s: (layer, 0, 0)),
                  pl.BlockSpec((1, ATTN_WIDTH), const),
                  pl.BlockSpec((1, KV_WIDTH), const),
                  pl.BlockSpec((ATTN_WIDTH, ATTN_WIDTH), const),
                  pl.BlockSpec((ts, LANES), lambda b, s: (s, 0)),
                  pl.BlockSpec((ts, LANES), lambda b, s: (s, 0))],
        out_specs=[pl.BlockSpec((1, ts, ATTN_WIDTH), lambda b, s: (b, s, 0)),
                   pl.BlockSpec((1, ts, KV_WIDTH), lambda b, s: (b, s, 0)),
                   pl.BlockSpec((1, ts, KV_WIDTH), lambda b, s: (b, s, 0)),
                   pl.BlockSpec((1, ts, cw), lambda b, s: (b, s, 0))],
        out_shape=[jax.ShapeDtypeStruct((B, S, ATTN_WIDTH), BF16),
                   jax.ShapeDtypeStruct((B, S, KV_WIDTH), BF16),
                   jax.ShapeDtypeStruct((B, S, KV_WIDTH), BF16),
                   jax.ShapeDtypeStruct((B, S, cw), F32)],
        scratch_shapes=[pltpu.VMEM((D, W), BF16)],
        compiler_params=_cparams("arbitrary", "arbitrary"),
        name="inproj",
    )(x, mod, g_mix.reshape(1, D), w_in, gq, gk, bd, cos_t, sin_t)


def _attn_kernel(q_ref, k_ref, v_ref, o_ref):
    k = k_ref[0]
    v = v_ref[0]
    tq = q_ref.shape[1]
    lane = lax.broadcasted_iota(I32, (tq, LANES), 1)
    low = lane < HEAD_DIM
    heads_per_tile = LANES // HEAD_DIM
    group = N_Q_HEADS // N_KV_HEADS

    def scores(h):
        j, hh = divmod(h, heads_per_tile)
        g = h // group
        qt = q_ref[0, :, j * LANES:(j + 1) * LANES].astype(F32)
        qm = jnp.where(low if hh == 0 else jnp.logical_not(low), qt, 0.0)
        if hh != g:
            qm = pltpu.roll(qm, HEAD_DIM, 1)
        return lax.dot_general(qm.astype(BF16), k, (((1,), (1,)), ((), ())),
                               preferred_element_type=F32)

    s_next = scores(0)
    placed = []
    for h in range(N_Q_HEADS):
        s = s_next
        if h + 1 < N_Q_HEADS:
            s_next = scores(h + 1)
        j, hh = divmod(h, heads_per_tile)
        m = jnp.max(s, axis=-1, keepdims=True)
        p = jnp.exp2(s - m)
        l = jnp.sum(p, axis=-1, keepdims=True)
        o = jnp.dot(p.astype(BF16), v, preferred_element_type=F32) / l
        if hh != h // group:
            o = pltpu.roll(o, HEAD_DIM, 1)
        placed.append(o)
        if hh == heads_per_tile - 1:
            o_ref[0, :, j * LANES:(j + 1) * LANES] = jnp.where(low, placed[0], placed[1]).astype(BF16)
            placed = []


def _attention(q, k, v, tq):
    B, S, _ = q.shape
    return pl.pallas_call(
        _attn_kernel,
        grid=(B, S // tq),
        in_specs=[pl.BlockSpec((1, tq, ATTN_WIDTH), lambda b, i: (b, i, 0)),
                  pl.BlockSpec((1, S, KV_WIDTH), lambda b, i: (b, 0, 0)),
                  pl.BlockSpec((1, S, KV_WIDTH), lambda b, i: (b, 0, 0))],
        out_specs=pl.BlockSpec((1, tq, ATTN_WIDTH), lambda b, i: (b, i, 0)),
        out_shape=jax.ShapeDtypeStruct((B, S, ATTN_WIDTH), BF16),
        compiler_params=_cparams("parallel", "parallel"),
        name="attention",
    )(q, k, v)


def _conv_kernel(u_ref, w_ref, b_ref, g_ref, beta_ref, o_ref, pad_ref, *, tc):
    S, C = u_ref.shape[1], u_ref.shape[2]
    halo = 2 * SUB
    pad_ref[0:halo, :] = jnp.zeros((halo, C), F32)
    pad_ref[halo + S:halo + S + halo, :] = jnp.zeros((halo, C), F32)
    pad_ref[halo:halo + S, :] = u_ref[0]
    span = tc + 2 * halo

    def tile(i, carry):
        s0 = pl.multiple_of(i * tc, tc)
        cols = []
        for c0 in range(0, C, LANES):
            win = pad_ref[pl.ds(s0, span), c0:c0 + LANES]
            acc = jnp.zeros((tc, LANES), F32)
            for r in range(SUB):
                shifted = win if r == 0 else pltpu.roll(win, span - r, 0)
                for j in range(CONV_WIDTH):
                    off = halo - CONV_PAD + j
                    if off % SUB == r:
                        a = off - r
                        acc = acc + shifted[a:a + tc, :] * w_ref[j:j + 1, c0:c0 + LANES]
            cols.append(acc)
        acc = jnp.concatenate(cols, axis=1) + b_ref[...]
        mu = jnp.mean(acc, axis=-1, keepdims=True)
        d = acc - mu
        var = jnp.mean(d * d, axis=-1, keepdims=True)
        y = d * lax.rsqrt(var + EPS) * g_ref[...] + beta_ref[...]
        o_ref[0, pl.ds(s0, tc), :] = (y * jax.nn.sigmoid(y)).astype(BF16)
        return carry

    lax.fori_loop(0, S // tc, tile, 0)


def _conv(u, w_dw, b_dw, g_cn, b_cn):
    B, S, C = u.shape
    tc = _tile_rows(S, (128, 64))
    const = lambda b: (0, 0)
    return pl.pallas_call(
        functools.partial(_conv_kernel, tc=tc),
        grid=(B,),
        in_specs=[pl.BlockSpec((1, S, C), lambda b: (b, 0, 0)),
                  pl.BlockSpec((CONV_WIDTH, C), const),
                  pl.BlockSpec((1, C), const),
                  pl.BlockSpec((1, C), const),
                  pl.BlockSpec((1, C), const)],
        out_specs=pl.BlockSpec((1, S, C), lambda b: (b, 0, 0)),
        out_shape=jax.ShapeDtypeStruct((B, S, C), BF16),
        scratch_shapes=[pltpu.VMEM((S + 4 * SUB, C), F32)],
        compiler_params=_cparams("parallel"),
        name="conv",
    )(u, w_dw, b_dw.reshape(1, C), g_cn.reshape(1, C), b_cn.reshape(1, C))


def _split_bf16(a):
    hi = a.astype(BF16)
    lo = (a - hi.astype(F32)).astype(BF16)
    return hi, lo


def _outproj_router_kernel(x_ref, attn_ref, conv_ref, wout_ref, mod_ref, gffn_ref, wr_ref, br_ref, tri_ref, etri_ref,
                           x1_ref, h2_ref, pos_ref, gp_ref, a8_ref, n8_ref, b8_ref, tot_ref, carry_ref, wb_ref):
    first_step = jnp.logical_and(pl.program_id(0) == 0, pl.program_id(1) == 0)

    @pl.when(first_step)
    def _():
        carry_ref[...] = jnp.zeros_like(carry_ref)
        wb_ref[...] = wout_ref[0].astype(BF16)

    mod = mod_ref[0]
    gt1, sh2, sc2 = mod[2:3], mod[3:4], mod[4:5]
    aw = attn_ref.shape[2]
    mix = (jnp.dot(attn_ref[0], wb_ref[:aw, :], preferred_element_type=F32)
           + jnp.dot(conv_ref[0], wb_ref[aw:, :], preferred_element_type=F32))
    x1 = x_ref[0] + gt1 * mix
    x1_ref[0] = x1
    ms = jnp.mean(x1 * x1, axis=-1, keepdims=True)
    h2 = x1 * lax.rsqrt(ms + EPS) * gffn_ref[...] * (1.0 + sc2) + sh2
    h2_ref[...] = h2.astype(BF16)

    h_hi, h_lo = _split_bf16(h2)
    w_hi, w_lo = _split_bf16(wr_ref[...])
    nt = (((1,), (1,)), ((), ()))
    logits = (lax.dot_general(w_hi, h_hi, nt, preferred_element_type=F32)
              + lax.dot_general(w_lo, h_hi, nt, preferred_element_type=F32)
              + lax.dot_general(w_hi, h_lo, nt, preferred_element_type=F32)) + br_ref[...]
    n_exp, ts = logits.shape
    reps = ts // LANES
    eio = lax.broadcasted_iota(I32, logits.shape, 0).astype(F32)
    vals, hits = [], []
    work = logits
    for _ in range(TOP_K):
        m = jnp.max(work, axis=0, keepdims=True)
        ik = jnp.min(jnp.where(work == m, eio, float(n_exp)), axis=0, keepdims=True)
        hit = eio == ik
        work = jnp.where(hit, -jnp.inf, work)
        vals.append(m)
        hits.append(hit)
    ex = [jnp.exp(v - vals[0]) for v in vals]
    den = ex[0] + ex[1] + ex[2] + ex[3]

    chosen = jnp.zeros(logits.shape, F32)
    for hit in hits:
        chosen = chosen + hit.astype(F32)
    prefix = jnp.dot(chosen.astype(BF16), tri_ref[...], preferred_element_type=F32)
    cnt = jnp.sum(chosen, axis=1, keepdims=True)
    n8 = jnp.broadcast_to(jnp.floor((cnt + (SUB - 1.0)) * (1.0 / SUB)), (n_exp, LANES))
    b8 = jnp.dot(etri_ref[...], n8.astype(BF16), preferred_element_type=F32)
    slot = jnp.concatenate([b8 * float(SUB)] * reps, axis=1) + prefix
    pos = [jnp.sum(jnp.where(hit, slot, 0.0), axis=0, keepdims=True) for hit in hits]
    for kk in range(TOP_K):
        pos_ref[kk:kk + 1, :] = pos[kk].astype(I32)
    a8_ref[0] = carry_ref[...].astype(I32)
    n8_ref[0] = n8.astype(I32)
    b8_ref[0] = b8.astype(I32)
    carry_ref[...] = carry_ref[...] + n8
    tot_ref[...] = carry_ref[...]

    rows = [e / den for e in ex] + pos + [jnp.zeros((LANES - 2 * TOP_K, ts), F32)]
    gp_ref[...] = jnp.concatenate(rows, axis=0).T


def _outproj_router(x, attn, conv, w_out, mod, layer, g_ffn, w_router, b_router, tri, etri, ts):
    B, S, D = x.shape
    N = B * S
    E = w_router.shape[1]
    nts = S // ts
    aw, cw = attn.shape[2], conv.shape[2]
    const = lambda b, s: (0, 0)
    tok = lambda b, s: (0, b * nts + s)
    tile3 = lambda b, s: (b * nts + s, 0, 0)
    return pl.pallas_call(
        _outproj_router_kernel,
        grid=(B, nts),
        in_specs=[pl.BlockSpec((1, ts, D), lambda b, s: (b, s, 0)),
                  pl.BlockSpec((1, ts, aw), lambda b, s: (b, s, 0)),
                  pl.BlockSpec((1, ts, cw), lambda b, s: (b, s, 0)),
                  pl.BlockSpec((1, aw + cw, D), lambda b, s: (layer, 0, 0)),
                  pl.BlockSpec((None, 1, 6, D), lambda b, s: (layer, b, 0, 0)),
                  pl.BlockSpec((1, D), const),
                  pl.BlockSpec((E, D), const),
                  pl.BlockSpec((E, 1), const),
                  pl.BlockSpec((ts, ts), const),
                  pl.BlockSpec((E, E), const)],
        out_specs=[pl.BlockSpec((1, ts, D), lambda b, s: (b, s, 0)),
                   pl.BlockSpec((ts, D), lambda b, s: (b * nts + s, 0)),
                   pl.BlockSpec((TOP_K, ts), tok),
                   pl.BlockSpec((ts, LANES), lambda b, s: (b * nts + s, 0)),
                   pl.BlockSpec((1, E, LANES), tile3),
                   pl.BlockSpec((1, E, LANES), tile3),
                   pl.BlockSpec((1, E, LANES), tile3),
                   pl.BlockSpec((E, LANES), const)],
        out_shape=[jax.ShapeDtypeStruct((B, S, D), F32),
                   jax.ShapeDtypeStruct((N, D), BF16),
                   jax.ShapeDtypeStruct((TOP_K, N), I32),
                   jax.ShapeDtypeStruct((N, LANES), F32),
                   jax.ShapeDtypeStruct((B * nts, E, LANES), I32),
                   jax.ShapeDtypeStruct((B * nts, E, LANES), I32),
                   jax.ShapeDtypeStruct((B * nts, E, LANES), I32),
                   jax.ShapeDtypeStruct((E, LANES), F32)],
        scratch_shapes=[pltpu.VMEM((E, LANES), F32), pltpu.VMEM((aw + cw, D), BF16)],
        compiler_params=_cparams("arbitrary", "arbitrary"),
        name="outproj_router",
    )(x, attn, conv, w_out, mod, g_ffn.reshape(1, D), w_router.T, b_router.reshape(E, 1), tri, etri)


def _start_segment_copies(n8_ref, a_ref, b_ref, base, n_exp, max_units, copy):
    n_bits = max_units.bit_length()
    common = min(4, n_bits)
    for e in range(n_exp):
        n = n8_ref[base + e]
        a0 = a_ref[base + e]
        b0 = b_ref[base + e]

        def chunk(bit):
            done = (n >> (bit + 1)) << (bit + 1)

            @pl.when(((n >> bit) & 1) == 1)
            def _():
                copy(pl.multiple_of((a0 + done) * SUB, SUB), pl.multiple_of((b0 + done) * SUB, SUB),
                     SUB << bit).start()

        for bit in reversed(range(common)):
            chunk(bit)

        @pl.when(n >= (1 << common))
        def _():
            for bit in reversed(range(common, n_bits)):
                chunk(bit)


def _wait_segment_copies(units, big_units, copy):
    shift = big_units.bit_length() - 1
    lax.fori_loop(0, units >> shift, lambda u, c: (copy(0, 0, SUB * big_units).wait(), c)[1], 0)
    lax.fori_loop(0, units & (big_units - 1), lambda u, c: (copy(0, 0, SUB).wait(), c)[1], 0)


def _dispatch_kernel(dst8_ref, n8_ref, src8_ref, units_ref, tail0_ref, tailn_ref, nu_ref,
                     pos_ref, h_ref, xs_ref, stage_ref, zero_ref, sem, zsem, *, chunk):
    j = pl.program_id(0)
    last = pl.num_programs(0) - 1
    slot = j % 2
    ts = h_ref.shape[0]
    n_exp = tail0_ref.shape[0]
    rows = stage_ref.shape[1]
    h = h_ref[...]
    pos = pos_ref[...]
    for r0 in range(0, rows, chunk):
        rio = lax.broadcasted_iota(I32, (chunk, ts), 0) + r0
        onehot = jnp.zeros((chunk, ts), F32)
        for kk in range(TOP_K):
            onehot = jnp.where(rio == pos[kk:kk + 1, :], 1.0, onehot)
        stage_ref[slot, r0:r0 + chunk, :] = _pack_bf16_pairs(jnp.dot(onehot.astype(BF16), h, preferred_element_type=F32))

    def seg_copy(sl):
        def build(src, dst, nrows):
            return pltpu.make_async_copy(stage_ref.at[sl, pl.ds(src, nrows)], xs_ref.at[pl.ds(dst, nrows)],
                                         sem.at[sl])
        return build

    _start_segment_copies(n8_ref, src8_ref, dst8_ref, j * n_exp, n_exp, ts // SUB, seg_copy(slot))

    @pl.when(j > 0)
    def _():
        _wait_segment_copies(units_ref[j - 1], ts // SUB, seg_copy(1 - slot))

    @pl.when(j == last)
    def _():
        zero_ref[...] = jnp.zeros_like(zero_ref)

        def zero_copy(u):
            return pltpu.make_async_copy(zero_ref.at[pl.ds(0, SUB)],
                                         xs_ref.at[pl.ds(pl.multiple_of(u * SUB, SUB), SUB)], zsem)

        for e in range(n_exp):
            lo = tail0_ref[e]
            hi = lo + tailn_ref[e]
            lax.fori_loop(lo, hi, lambda u, c: (zero_copy(u).start(), c)[1], 0)
            lax.fori_loop(lo, hi, lambda u, c: (zero_copy(u).wait(), c)[1], 0)

        tm = zero_ref.shape[0]

        def zero_block(b):
            return pltpu.make_async_copy(zero_ref, xs_ref.at[pl.ds(pl.multiple_of(b * tm, tm), tm)], zsem)

        n_blocks = xs_ref.shape[0] // tm
        lax.fori_loop(nu_ref[0], n_blocks, lambda b, c: (zero_block(b).start(), c)[1], 0)
        lax.fori_loop(nu_ref[0], n_blocks, lambda b, c: (zero_block(b).wait(), c)[1], 0)
        _wait_segment_copies(units_ref[j], ts // SUB, seg_copy(slot))


def _dispatch(dst8, n8, src8, units, tail0, tailn, n_used, pos, h2, n_rows, ts, tm):
    N, D = h2.shape
    E = tail0.shape[0]
    rows = _stage_rows(ts, E)
    chunk = 256 if rows % 256 == 0 else LANES
    return pl.pallas_call(
        functools.partial(_dispatch_kernel, chunk=chunk),
        grid_spec=pltpu.PrefetchScalarGridSpec(
            num_scalar_prefetch=7,
            grid=(N // ts,),
            in_specs=[pl.BlockSpec((TOP_K, ts), lambda i, *_: (0, i)),
                      pl.BlockSpec((ts, D), lambda i, *_: (i, 0))],
            out_specs=pl.BlockSpec(memory_space=pl.ANY),
            scratch_shapes=[pltpu.VMEM((2, rows, D // 2), U32),
                            pltpu.VMEM((tm, D // 2), U32),
                            pltpu.SemaphoreType.DMA((2,)),
                            pltpu.SemaphoreType.DMA]),
        out_shape=jax.ShapeDtypeStruct((n_rows, D // 2), U32),
        compiler_params=_cparams("arbitrary"),
        name="dispatch",
    )(dst8, n8, src8, units, tail0, tailn, n_used, pos, h2)


def _expert_kernel(be_ref, nu_ref, nx_ref, par_ref, x_ref, b1_ref, b2_ref, w1_hbm, w2_hbm, y_ref,
                   w1f_ref, w2f_ref, w1b_ref, w2b_ref, sem, *, layer):
    i = pl.program_id(0)
    live = i < nu_ref[0]
    e = be_ref[i]
    fresh = jnp.logical_or(i == 0, e != be_ref[jnp.maximum(i - 1, 0)])
    slot = par_ref[i]

    def weight_copies(expert, sl):
        return (pltpu.make_async_copy(w1_hbm.at[layer, expert], w1f_ref.at[sl], sem.at[0, sl]),
                pltpu.make_async_copy(w2_hbm.at[layer, expert], w2f_ref.at[sl], sem.at[1, sl]))

    @pl.when(i == 0)
    def _():
        for cp in weight_copies(e, slot):
            cp.start()

    @pl.when(jnp.logical_and(live, fresh))
    def _():
        for cp in weight_copies(e, slot):
            cp.wait()
        w1b_ref[...] = w1f_ref[slot].astype(BF16)
        w2b_ref[...] = w2f_ref[slot].astype(BF16)
        nxt = nx_ref[i]

        @pl.when(nxt >= 0)
        def _():
            for cp in weight_copies(nxt, 1 - slot):
                cp.start()

    @pl.when(live)
    def _():
        dx = w2b_ref.shape[0]
        x = jnp.concatenate(_unpack_bf16_pairs(x_ref[...]), axis=1)
        hu = jnp.dot(x, w1b_ref[...], preferred_element_type=F32) + b1_ref[0, 0]
        x_glu = jnp.minimum(hu[:, :dx], SWIGLU_LIMIT)
        x_lin = jnp.clip(hu[:, dx:], -SWIGLU_LIMIT, SWIGLU_LIMIT)
        act = (x_lin + 1.0) * (x_glu * jax.nn.sigmoid(SWIGLU_ALPHA * x_glu))
        y = jnp.dot(act.astype(BF16), w2b_ref[...], preferred_element_type=F32) + b2_ref[0, 0]
        y_ref[...] = _pack_bf16_pairs(y)

    @pl.when(jnp.logical_not(live))
    def _():
        y_ref[...] = jnp.zeros_like(y_ref)


def _experts(block_e, n_used, next_e, parity, xs, w1, b1, w2, b2, layer, tm):
    P = xs.shape[0]
    _, E, D, H = w1.shape
    dx = w2.shape[2]
    row = lambda i, be, nu, nx, par: (jnp.minimum(i, nu[0] - 1), 0)
    exp4 = lambda i, be, nu, nx, par: (layer, be[i], 0, 0)
    return pl.pallas_call(
        functools.partial(_expert_kernel, layer=layer),
        grid_spec=pltpu.PrefetchScalarGridSpec(
            num_scalar_prefetch=4,
            grid=(P // tm,),
            in_specs=[pl.BlockSpec((tm, D // 2), row),
                      pl.BlockSpec((1, 1, 1, H), exp4),
                      pl.BlockSpec((1, 1, 1, D), exp4),
                      pl.BlockSpec(memory_space=pl.ANY),
                      pl.BlockSpec(memory_space=pl.ANY)],
            out_specs=pl.BlockSpec((tm, D // 2), lambda i, *_: (i, 0)),
            scratch_shapes=[pltpu.VMEM((2, D, H), F32), pltpu.VMEM((2, dx, D), F32),
                            pltpu.VMEM((D, H), BF16), pltpu.VMEM((dx, D), BF16),
                            pltpu.SemaphoreType.DMA((2, 2))]),
        out_shape=jax.ShapeDtypeStruct((P, D // 2), U32),
        compiler_params=_cparams("arbitrary"),
        name="experts",
    )(block_e, n_used, next_e, parity, xs,
      b1.reshape(b1.shape[0], E, 1, H), b2.reshape(b2.shape[0], E, 1, D), w1, w2)


def _combine_kernel(dst8_ref, n8_ref, src8_ref, units_ref,
                    x_ref, mod_ref, gp_ref, gfin_ref, ys_ref, o_ref, stage_ref, sem, *, final, chunk):
    j = pl.program_id(0) * pl.num_programs(1) + pl.program_id(1)
    n_tiles = pl.num_programs(0) * pl.num_programs(1)
    slot = j % 2
    ts = x_ref.shape[1]
    rows = stage_ref.shape[1]
    n_exp = (rows - ts * TOP_K) // SUB

    def seg_copy(sl):
        def build(src, dst, nrows):
            return pltpu.make_async_copy(ys_ref.at[pl.ds(dst, nrows)], stage_ref.at[sl, pl.ds(src, nrows)],
                                         sem.at[sl])
        return build

    def start_fetch(tile, sl):
        stage_ref[sl, ts * TOP_K:rows, :] = jnp.zeros((rows - ts * TOP_K, stage_ref.shape[2]), U32)
        _start_segment_copies(n8_ref, src8_ref, dst8_ref, tile * n_exp, n_exp, ts // SUB, seg_copy(sl))

    @pl.when(j == 0)
    def _():
        start_fetch(0, 0)

    @pl.when(j + 1 < n_tiles)
    def _():
        start_fetch(j + 1, 1 - slot)

    gp = gp_ref[...]
    gates = [jnp.broadcast_to(gp[:, kk:kk + 1], (ts, chunk)) for kk in range(TOP_K)]
    slots = [jnp.broadcast_to(gp[:, TOP_K + kk:TOP_K + kk + 1].astype(I32), (ts, chunk)) for kk in range(TOP_K)]
    _wait_segment_copies(units_ref[j], ts // SUB, seg_copy(slot))

    half = stage_ref.shape[2]
    moe_lo = jnp.zeros((ts, half), F32)
    moe_hi = jnp.zeros((ts, half), F32)
    for r0 in range(0, rows, chunk):
        rio = lax.broadcasted_iota(I32, (ts, chunk), 1) + r0
        wgt = jnp.zeros((ts, chunk), F32)
        for kk in range(TOP_K):
            wgt = jnp.where(rio == slots[kk], gates[kk], wgt)
        lo, hi = _unpack_bf16_pairs(stage_ref[slot, r0:r0 + chunk, :])
        wgt = wgt.astype(BF16)
        moe_lo = moe_lo + jnp.dot(wgt, lo, preferred_element_type=F32)
        moe_hi = moe_hi + jnp.dot(wgt, hi, preferred_element_type=F32)
    moe = jnp.concatenate([moe_lo, moe_hi], axis=1)
    gt2 = mod_ref[0][5:6]
    out = x_ref[0] + gt2 * moe
    if final:
        ms = jnp.mean(out * out, axis=-1, keepdims=True)
        out = out * lax.rsqrt(ms + EPS) * gfin_ref[...]
    o_ref[0] = out


def _combine(dst8, n8, src8, units, x1, mod, layer, gp, g_final, ys, n_exp, ts, final):
    B, S, D = x1.shape
    nts = S // ts
    rows = _stage_rows(ts, n_exp)
    chunk = 256 if rows % 256 == 0 else LANES
    return pl.pallas_call(
        functools.partial(_combine_kernel, final=final, chunk=chunk),
        grid_spec=pltpu.PrefetchScalarGridSpec(
            num_scalar_prefetch=4,
            grid=(B, nts),
            in_specs=[pl.BlockSpec((1, ts, D), lambda b, s, *_: (b, s, 0)),
                      pl.BlockSpec((None, 1, 6, D), lambda b, s, *_: (layer, b, 0, 0)),
                      pl.BlockSpec((ts, LANES), lambda b, s, *_: (b * nts + s, 0)),
                      pl.BlockSpec((1, D), lambda b, s, *_: (0, 0)),
                      pl.BlockSpec(memory_space=pl.ANY)],
            out_specs=pl.BlockSpec((1, ts, D), lambda b, s, *_: (b, s, 0)),
            scratch_shapes=[pltpu.VMEM((2, rows, D // 2), U32), pltpu.SemaphoreType.DMA((2,))]),
        out_shape=jax.ShapeDtypeStruct((B, S, D), F32),
        compiler_params=_cparams("arbitrary", "arbitrary"),
        name="combine",
    )(dst8, n8, src8, units, x1, mod, gp, g_final.reshape(1, D), ys)


def _rope_tables(seq_len):
    freqs = ROPE_THETA ** (-jnp.arange(ROPE_FREQS, dtype=F32) / ROPE_FREQS)
    pos = jnp.arange(seq_len, dtype=I32)
    row_ang = (pos // GRID_W).astype(F32)[:, None] * freqs
    col_ang = (pos % GRID_W).astype(F32)[:, None] * freqs
    cos = jnp.concatenate([jnp.cos(row_ang)] * 2 + [jnp.cos(col_ang)] * 2, axis=1)
    sin = jnp.concatenate([-jnp.sin(row_ang), jnp.sin(row_ang), -jnp.sin(col_ang), jnp.sin(col_ang)], axis=1)
    reps = LANES // HEAD_DIM
    return jnp.tile(cos, (1, reps)), jnp.tile(sin, (1, reps))


def _tile_rows(n, prefs):
    for t in prefs:
        if n % t == 0:
            return t
    return n


def kernel(x, c, w_mod, b_mod, g_mix, w_in, g_q, g_k, w_dw, b_dw, g_cn, b_cn,
           w_out, g_ffn, w_router, b_router, w1, b1, w2, b2, g_final):
    B, S, D = x.shape
    L = w_mod.shape[0]
    E = w_router.shape[2]
    N = B * S
    ts = _tile_rows(S, (512, 256, 128))
    tq = _tile_rows(S, (256, 128))
    tm = 512
    n_tiles = N // ts
    max_rows = N * TOP_K + n_tiles * E * (SUB - 1)
    n_rows = (max_rows + tm - 1) // tm * tm + E * tm
    n_blocks = n_rows // tm

    mod = _modulation(c, w_mod, b_mod).reshape(L, B, 6, D)
    cos_t, sin_t = _rope_tables(S)
    head = jnp.arange(ATTN_WIDTH, dtype=I32) // HEAD_DIM
    bd = jnp.where(head[:, None] == head[None, :], 1.0 / HEAD_DIM, 0.0).astype(BF16)
    tok = jnp.arange(ts, dtype=I32)
    tri = (tok[:, None] < tok[None, :]).astype(BF16)
    eid = jnp.arange(E, dtype=I32)
    etri = (eid[None, :] < eid[:, None]).astype(BF16)

    for l in range(L):
        q, k, v, u = _inproj(x, mod, l, g_mix[l], w_in, g_q[l], g_k[l], bd, cos_t, sin_t, ts)
        attn = _attention(q, k, v, tq)
        conv = _conv(u, w_dw[l], b_dw[l], g_cn[l], b_cn[l])
        x1, h2, pos, gp, a8, n8, b8, tot = _outproj_router(
            x, attn, conv, w_out, mod, l, g_ffn[l], w_router[l], b_router[l], tri, etri, ts)

        rows_e = tot[:, 0].astype(I32) * SUB
        padded = (rows_e + tm - 1) // tm * tm
        pend = jnp.cumsum(padded)
        pstart = pend - padded
        n_used = (pend[-1] // tm).astype(I32)
        blk = jnp.minimum(jnp.arange(n_blocks, dtype=I32), n_used - 1) * tm
        block_e = jnp.minimum(jnp.sum(blk[:, None] >= pend[None, :], axis=1), E - 1).astype(I32)
        dst8 = (a8[:, :, 0] + (pstart // SUB)[None, :]).reshape(-1)
        n8f = n8[:, :, 0].reshape(-1)
        src8 = b8[:, :, 0].reshape(-1)
        units = jnp.sum(n8[:, :, 0], axis=1)
        tail0 = (pstart + rows_e) // SUB
        tailn = (padded - rows_e) // SUB

        has = padded > 0
        ordinal = jnp.cumsum(has.astype(I32)) - 1
        later = jnp.where(jnp.logical_and(has[None, :], eid[None, :] > eid[:, None]), eid[None, :], E)
        nxt = jnp.min(later, axis=1)
        next_e = jnp.where(nxt < E, nxt, -1).astype(I32)[block_e]
        parity = (ordinal % 2).astype(I32)[block_e]

        n_used = n_used.reshape(1)
        xs = _dispatch(dst8, n8f, src8, units, tail0, tailn, n_used, pos, h2, n_rows, ts, tm)
        ys = _experts(block_e, n_used, next_e, parity, xs, w1, b1, w2, b2, l, tm)
        x = _combine(dst8, n8f, src8, units, x1, mod, l, gp, g_final, ys, E, ts, final=(l == L - 1))
    return x
```

```python
import functools

import jax
import jax.numpy as jnp
from jax import lax
from jax.experimental import pallas as pl
from jax.experimental.pallas import tpu as pltpu

F32 = jnp.float32
BF16 = jnp.bfloat16
I32 = jnp.int32

HEAD_DIM = 64
N_Q_HEADS = 8
N_KV_HEADS = 2
ATTN_WIDTH = N_Q_HEADS * HEAD_DIM
KV_WIDTH = N_KV_HEADS * HEAD_DIM
GRID_W = 64
ROPE_THETA = 10000.0
ROPE_FREQS = HEAD_DIM // 4
CONV_WIDTH = 31
CONV_PAD = CONV_WIDTH // 2
TOP_K = 4
SWIGLU_LIMIT = 7.0
SWIGLU_ALPHA = 1.702
EPS = 1e-6
LOG2E = 1.4426950408889634

LANES = 128
SUB = 8
VMEM_LIMIT = 56 * 1024 * 1024


def _cparams(*sem):
    return pltpu.CompilerParams(dimension_semantics=sem, vmem_limit_bytes=VMEM_LIMIT)


def _stage_rows(ts, n_exp):
    return ts * TOP_K + n_exp * SUB


def _mod_kernel(c_ref, w_ref, b_ref, o_ref):
    c = c_ref[...]
    c_act = (c * jax.nn.sigmoid(c)).astype(BF16)
    o_ref[0] = jnp.dot(c_act, w_ref[0].astype(BF16), preferred_element_type=F32) + b_ref[0]


def _modulation(c, w_mod, b_mod):
    L, D, W = w_mod.shape
    B = c.shape[0]
    tn = 1536
    return pl.pallas_call(
        _mod_kernel,
        grid=(L, W // tn),
        in_specs=[pl.BlockSpec((B, D), lambda l, j: (0, 0)),
                  pl.BlockSpec((1, D, tn), lambda l, j: (l, 0, j)),
                  pl.BlockSpec((1, 1, tn), lambda l, j: (l, 0, j))],
        out_specs=pl.BlockSpec((1, B, tn), lambda l, j: (l, 0, j)),
        out_shape=jax.ShapeDtypeStruct((L, B, W), F32),
        compiler_params=_cparams("parallel", "parallel"),
        name="modulation",
    )(c, w_mod, b_mod.reshape(L, 1, W))


def _head_rmsnorm(t, gain, bd):
    msq = jnp.dot((t * t).astype(BF16), bd, preferred_element_type=F32)
    return t * lax.rsqrt(msq + EPS) * gain


def _rope(t, cos, sin):
    w = t.shape[1]
    reps = w // LANES
    cosw = jnp.concatenate([cos] * reps, axis=1) if reps > 1 else cos
    sinw = jnp.concatenate([sin] * reps, axis=1) if reps > 1 else sin
    lane = lax.broadcasted_iota(I32, t.shape, 1)
    first = (lane & ROPE_FREQS) == 0
    partner = jnp.where(first, pltpu.roll(t, w - ROPE_FREQS, 1), pltpu.roll(t, ROPE_FREQS, 1))
    return t * cosw + partner * sinw


def _inproj_kernel(x_ref, mod_ref, gmix_ref, w_ref, gq_ref, gk_ref, bd_ref, cos_ref, sin_ref,
                   q_ref, k_ref, v_ref, u_ref, wb_ref):
    @pl.when(jnp.logical_and(pl.program_id(0) == 0, pl.program_id(1) == 0))
    def _():
        wb_ref[...] = w_ref[0].astype(BF16)

    x = x_ref[0]
    mod = mod_ref[0]
    sh1, sc1 = mod[0:1], mod[1:2]
    ms = jnp.mean(x * x, axis=-1, keepdims=True)
    xn = x * lax.rsqrt(ms + EPS) * gmix_ref[...]
    h = (xn * (1.0 + sc1) + sh1).astype(BF16)
    proj = jnp.dot(h, wb_ref[...], preferred_element_type=F32)
    a0, a1, a2 = ATTN_WIDTH, ATTN_WIDTH + KV_WIDTH, ATTN_WIDTH + 2 * KV_WIDTH
    cw = (proj.shape[1] - a2) // 2
    q, k, v = proj[:, :a0], proj[:, a0:a1], proj[:, a1:a2]
    ca, cg = proj[:, a2:a2 + cw], proj[:, a2 + cw:]
    cos, sin = cos_ref[...], sin_ref[...]
    bd = bd_ref[...]
    q = _rope(_head_rmsnorm(q, gq_ref[...], bd), cos, sin)
    k = _rope(_head_rmsnorm(k, gk_ref[...], bd[:KV_WIDTH, :KV_WIDTH]), cos, sin)
    q_ref[0] = (q * (HEAD_DIM ** -0.5 * LOG2E)).astype(BF16)
    k_ref[0] = k.astype(BF16)
    v_ref[0] = v.astype(BF16)
    u_ref[0] = ca * jax.nn.sigmoid(cg)


def _inproj(x, mod, layer, g_mix, w_in, g_q, g_k, bd, cos_t, sin_t, ts):
    B, S, D = x.shape
    W = w_in.shape[2]
    cw = (W - ATTN_WIDTH - 2 * KV_WIDTH) // 2
    gq = jnp.tile(g_q, N_Q_HEADS).reshape(1, ATTN_WIDTH)
    gk = jnp.tile(g_k, N_KV_HEADS).reshape(1, KV_WIDTH)
    const = lambda b, s: (0, 0)
    return pl.pallas_call(
        _inproj_kernel,
        grid=(B, S // ts),
        in_specs=[pl.BlockSpec((1, ts, D), lambda b, s: (b, s, 0)),
                  pl.BlockSpec((None, 1, 6, D), lambda b, s: (layer, b, 0, 0)),
                  pl.BlockSpec((1, D), const),
                  pl.BlockSpec((1, D, W), lambda b, s: (layer, 0, 0)),
                  pl.BlockSpec((1, ATTN_WIDTH), const),
                  pl.BlockSpec((1, KV_WIDTH), const),
                  pl.BlockSpec((ATTN_WIDTH, ATTN_WIDTH), const),
                  pl.BlockSpec((ts, LANES), lambda b, s: (s, 0)),
                  pl.BlockSpec((ts, LANES), lambda b, s: (s, 0))],
        out_specs=[pl.BlockSpec((1, ts, ATTN_WIDTH), lambda b, s: (b, s, 0)),
                   pl.BlockSpec((1, ts, KV_WIDTH), lambda b, s: (b, s, 0)),
                   pl.BlockSpec((1, ts, KV_WIDTH), lambda b, s: (b, s, 0)),
                   pl.BlockSpec((1, ts, cw), lambda b, s: (b, s, 0))],
        out_shape=[jax.ShapeDtypeStruct((B, S, ATTN_WIDTH), BF16),
                   jax.ShapeDtypeStruct((B, S, KV_WIDTH), BF16),
                   jax.ShapeDtypeStruct((B, S, KV_WIDTH), BF16),
                   jax.ShapeDtypeStruct((B, S, cw), F32)],
        scratch_shapes=[pltpu.VMEM((D, W), BF16)],
        compiler_params=_cparams("arbitrary", "arbitrary"),
        name="inproj",
    )(x, mod, g_mix.reshape(1, D), w_in, gq, gk, bd, cos_t, sin_t)


def _attn_kernel(q_ref, k_ref, v_ref, o_ref):
    k = k_ref[0]
    v = v_ref[0]
    tq = q_ref.shape[1]
    lane = lax.broadcasted_iota(I32, (tq, LANES), 1)
    low = lane < HEAD_DIM
    heads_per_tile = LANES // HEAD_DIM
    group = N_Q_HEADS // N_KV_HEADS

    def scores(h):
        j, hh = divmod(h, heads_per_tile)
        g = h // group
        qt = q_ref[0, :, j * LANES:(j + 1) * LANES].astype(F32)
        qm = jnp.where(low if hh == 0 else jnp.logical_not(low), qt, 0.0)
        if hh != g:
            qm = pltpu.roll(qm, HEAD_DIM, 1)
        return lax.dot_general(qm.astype(BF16), k, (((1,), (1,)), ((), ())),
                               preferred_element_type=F32)

    s_next = scores(0)
    placed = []
    for h in range(N_Q_HEADS):
        s = s_next
        if h + 1 < N_Q_HEADS:
            s_next = scores(h + 1)
        j, hh = divmod(h, heads_per_tile)
        m = jnp.max(s, axis=-1, keepdims=True)
        p = jnp.exp2(s - m)
        l = jnp.sum(p, axis=-1, keepdims=True)
        o = jnp.dot(p.astype(BF16), v, preferred_element_type=F32) / l
        if hh != h // group:
            o = pltpu.roll(o, HEAD_DIM, 1)
        placed.append(o)
        if hh == heads_per_tile - 1:
            o_ref[0, :, j * LANES:(j + 1) * LANES] = jnp.where(low, placed[0], placed[1]).astype(BF16)
            placed = []


def _attention(q, k, v, tq):
    B, S, _ = q.shape
    return pl.pallas_call(
        _attn_kernel,
        grid=(B, S // tq),
        in_specs=[pl.BlockSpec((1, tq, ATTN_WIDTH), lambda b, i: (b, i, 0)),
                  pl.BlockSpec((1, S, KV_WIDTH), lambda b, i: (b, 0, 0)),
                  pl.BlockSpec((1, S, KV_WIDTH), lambda b, i: (b, 0, 0))],
        out_specs=pl.BlockSpec((1, tq, ATTN_WIDTH), lambda b, i: (b, i, 0)),
        out_shape=jax.ShapeDtypeStruct((B, S, ATTN_WIDTH), BF16),
        compiler_params=_cparams("parallel", "parallel"),
        name="attention",
    )(q, k, v)


def _conv_kernel(u_ref, w_ref, b_ref, g_ref, beta_ref, o_ref, pad_ref, *, tc):
    S, C = u_ref.shape[1], u_ref.shape[2]
    halo = 2 * SUB
    pad_ref[0:halo, :] = jnp.zeros((halo, C), F32)
    pad_ref[halo + S:halo + S + halo, :] = jnp.zeros((halo, C), F32)
    pad_ref[halo:halo + S, :] = u_ref[0]
    span = tc + 2 * halo

    def tile(i, carry):
        s0 = pl.multiple_of(i * tc, tc)
        cols = []
        for c0 in range(0, C, LANES):
            win = pad_ref[pl.ds(s0, span), c0:c0 + LANES]
            acc = jnp.zeros((tc, LANES), F32)
            for r in range(SUB):
                shifted = win if r == 0 else pltpu.roll(win, span - r, 0)
                for j in range(CONV_WIDTH):
                    off = halo - CONV_PAD + j
                    if off % SUB == r:
                        a = off - r
                        acc = acc + shifted[a:a + tc, :] * w_ref[j:j + 1, c0:c0 + LANES]
            cols.append(acc)
        acc = jnp.concatenate(cols, axis=1) + b_ref[...]
        mu = jnp.mean(acc, axis=-1, keepdims=True)
        d = acc - mu
        var = jnp.mean(d * d, axis=-1, keepdims=True)
        y = d * lax.rsqrt(var + EPS) * g_ref[...] + beta_ref[...]
        o_ref[0, pl.ds(s0, tc), :] = (y * jax.nn.sigmoid(y)).astype(BF16)
        return carry

    lax.fori_loop(0, S // tc, tile, 0)


def _conv(u, w_dw, b_dw, g_cn, b_cn):
    B, S, C = u.shape
    tc = _tile_rows(S, (128, 64))
    const = lambda b: (0, 0)
    return pl.pallas_call(
        functools.partial(_conv_kernel, tc=tc),
        grid=(B,),
        in_specs=[pl.BlockSpec((1, S, C), lambda b: (b, 0, 0)),
                  pl.BlockSpec((CONV_WIDTH, C), const),
                  pl.BlockSpec((1, C), const),
                  pl.BlockSpec((1, C), const),
                  pl.BlockSpec((1, C), const)],
        out_specs=pl.BlockSpec((1, S, C), lambda b: (b, 0, 0)),
        out_shape=jax.ShapeDtypeStruct((B, S, C), BF16),
        scratch_shapes=[pltpu.VMEM((S + 4 * SUB, C), F32)],
        compiler_params=_cparams("parallel"),
        name="conv",
    )(u, w_dw, b_dw.reshape(1, C), g_cn.reshape(1, C), b_cn.reshape(1, C))


def _split_bf16(a):
    hi = a.astype(BF16)
    lo = (a - hi.astype(F32)).astype(BF16)
    return hi, lo


def _outproj_router_kernel(x_ref, attn_ref, conv_ref, wout_ref, mod_ref, gffn_ref, wr_ref, br_ref, tri_ref, etri_ref,
                           x1_ref, h2_ref, pos_ref, gp_ref, a8_ref, n8_ref, b8_ref, tot_ref, carry_ref, wb_ref):
    first_step = jnp.logical_and(pl.program_id(0) == 0, pl.program_id(1) == 0)

    @pl.when(first_step)
    def _():
        carry_ref[...] = jnp.zeros_like(carry_ref)
        wb_ref[...] = wout_ref[0].astype(BF16)

    mod = mod_ref[0]
    gt1, sh2, sc2 = mod[2:3], mod[3:4], mod[4:5]
    aw = attn_ref.shape[2]
    mix = (jnp.dot(attn_ref[0], wb_ref[:aw, :], preferred_element_type=F32)
           + jnp.dot(conv_ref[0], wb_ref[aw:, :], preferred_element_type=F32))
    x1 = x_ref[0] + gt1 * mix
    x1_ref[0] = x1
    ms = jnp.mean(x1 * x1, axis=-1, keepdims=True)
    h2 = x1 * lax.rsqrt(ms + EPS) * gffn_ref[...] * (1.0 + sc2) + sh2
    h2_ref[...] = h2.astype(BF16)

    h_hi, h_lo = _split_bf16(h2)
    w_hi, w_lo = _split_bf16(wr_ref[...])
    nt = (((1,), (1,)), ((), ()))
    logits = (lax.dot_general(w_hi, h_hi, nt, preferred_element_type=F32)
              + lax.dot_general(w_lo, h_hi, nt, preferred_element_type=F32)
              + lax.dot_general(w_hi, h_lo, nt, preferred_element_type=F32)) + br_ref[...]
    n_exp, ts = logits.shape
    reps = ts // LANES
    eio = lax.broadcasted_iota(I32, logits.shape, 0).astype(F32)
    vals, hits = [], []
    work = logits
    for _ in range(TOP_K):
        m = jnp.max(work, axis=0, keepdims=True)
        ik = jnp.min(jnp.where(work == m, eio, float(n_exp)), axis=0, keepdims=True)
        hit = eio == ik
        work = jnp.where(hit, -jnp.inf, work)
        vals.append(m)
        hits.append(hit)
    ex = [jnp.exp(v - vals[0]) for v in vals]
    den = ex[0] + ex[1] + ex[2] + ex[3]

    chosen = jnp.zeros(logits.shape, F32)
    for hit in hits:
        chosen = chosen + hit.astype(F32)
    prefix = jnp.dot(chosen.astype(BF16), tri_ref[...], preferred_element_type=F32)
    cnt = jnp.sum(chosen, axis=1, keepdims=True)
    n8 = jnp.broadcast_to(jnp.floor((cnt + (SUB - 1.0)) * (1.0 / SUB)), (n_exp, LANES))
    b8 = jnp.dot(etri_ref[...], n8.astype(BF16), preferred_element_type=F32)
    slot = jnp.concatenate([b8 * float(SUB)] * reps, axis=1) + prefix
    pos = [jnp.sum(jnp.where(hit, slot, 0.0), axis=0, keepdims=True) for hit in hits]
    for kk in range(TOP_K):
        pos_ref[kk:kk + 1, :] = pos[kk].astype(I32)
    a8_ref[0] = carry_ref[...].astype(I32)
    n8_ref[0] = n8.astype(I32)
    b8_ref[0] = b8.astype(I32)
    carry_ref[...] = carry_ref[...] + n8
    tot_ref[...] = carry_ref[...]

    rows = [e / den for e in ex] + pos + [jnp.zeros((LANES - 2 * TOP_K, ts), F32)]
    gp_ref[...] = jnp.concatenate(rows, axis=0).T


def _outproj_router(x, attn, conv, w_out, mod, layer, g_ffn, w_router, b_router, tri, etri, ts):
    B, S, D = x.shape
    N = B * S
    E = w_router.shape[1]
    nts = S // ts
    aw, cw = attn.shape[2], conv.shape[2]
    const = lambda b, s: (0, 0)
    tok = lambda b, s: (0, b * nts + s)
    tile3 = lambda b, s: (b * nts + s, 0, 0)
    return pl.pallas_call(
        _outproj_router_kernel,
        grid=(B, nts),
        in_specs=[pl.BlockSpec((1, ts, D), lambda b, s: (b, s, 0)),
                  pl.BlockSpec((1, ts, aw), lambda b, s: (b, s, 0)),
                  pl.BlockSpec((1, ts, cw), lambda b, s: (b, s, 0)),
                  pl.BlockSpec((1, aw + cw, D), lambda b, s: (layer, 0, 0)),
                  pl.BlockSpec((None, 1, 6, D), lambda b, s: (layer, b, 0, 0)),
                  pl.BlockSpec((1, D), const),
                  pl.BlockSpec((E, D), const),
                  pl.BlockSpec((E, 1), const),
                  pl.BlockSpec((ts, ts), const),
                  pl.BlockSpec((E, E), const)],
        out_specs=[pl.BlockSpec((1, ts, D), lambda b, s: (b, s, 0)),
                   pl.BlockSpec((ts, D), lambda b, s: (b * nts + s, 0)),
                   pl.BlockSpec((TOP_K, ts), tok),
                   pl.BlockSpec((ts, LANES), lambda b, s: (b * nts + s, 0)),
                   pl.BlockSpec((1, E, LANES), tile3),
                   pl.BlockSpec((1, E, LANES), tile3),
                   pl.BlockSpec((1, E, LANES), tile3),
                   pl.BlockSpec((E, LANES), const)],
        out_shape=[jax.ShapeDtypeStruct((B, S, D), F32),
                   jax.ShapeDtypeStruct((N, D), BF16),
                   jax.ShapeDtypeStruct((TOP_K, N), I32),
                   jax.ShapeDtypeStruct((N, LANES), F32),
                   jax.ShapeDtypeStruct((B * nts, E, LANES), I32),
                   jax.ShapeDtypeStruct((B * nts, E, LANES), I32),
                   jax.ShapeDtypeStruct((B * nts, E, LANES), I32),
                   jax.ShapeDtypeStruct((E, LANES), F32)],
        scratch_shapes=[pltpu.VMEM((E, LANES), F32), pltpu.VMEM((aw + cw, D), BF16)],
        compiler_params=_cparams("arbitrary", "arbitrary"),
        name="outproj_router",
    )(x, attn, conv, w_out, mod, g_ffn.reshape(1, D), w_router.T, b_router.reshape(E, 1), tri, etri)


def _start_segment_copies(n8_ref, a_ref, b_ref, base, n_exp, max_units, copy):
    n_bits = max_units.bit_length()
    common = min(4, n_bits)
    for e in range(n_exp):
        n = n8_ref[base + e]
        a0 = a_ref[base + e]
        b0 = b_ref[base + e]

        def chunk(bit):
            done = (n >> (bit + 1)) << (bit + 1)

            @pl.when(((n >> bit) & 1) == 1)
            def _():
                copy(pl.multiple_of((a0 + done) * SUB, SUB), pl.multiple_of((b0 + done) * SUB, SUB),
                     SUB << bit).start()

        for bit in reversed(range(common)):
            chunk(bit)

        @pl.when(n >= (1 << common))
        def _():
            for bit in reversed(range(common, n_bits)):
                chunk(bit)


def _wait_segment_copies(units, big_units, copy):
    shift = big_units.bit_length() - 1
    lax.fori_loop(0, units >> shift, lambda u, c: (copy(0, 0, SUB * big_units).wait(), c)[1], 0)
    lax.fori_loop(0, units & (big_units - 1), lambda u, c: (copy(0, 0, SUB).wait(), c)[1], 0)


def _dispatch_kernel(pos_ref, h_ref, st_ref, *, chunk):
    ts = h_ref.shape[0]
    rows = st_ref.shape[1]
    is_tile = pl.program_id(0) < pl.num_programs(0) - 1

    @pl.when(is_tile)
    def _():
        h = h_ref[...]
        pos = pos_ref[...]
        for r0 in range(0, rows, chunk):
            rio = lax.broadcasted_iota(I32, (chunk, ts), 0) + r0
            onehot = jnp.zeros((chunk, ts), F32)
            for kk in range(TOP_K):
                onehot = jnp.where(rio == pos[kk:kk + 1, :], 1.0, onehot)
            st_ref[0, r0:r0 + chunk, :] = jnp.dot(onehot.astype(BF16), h, preferred_element_type=F32)

    @pl.when(jnp.logical_not(is_tile))
    def _():
        st_ref[...] = jnp.zeros_like(st_ref)


def _dispatch(pos, h2, n_exp, ts):
    N, D = h2.shape
    n_tiles = N // ts
    rows = _stage_rows(ts, n_exp)
    chunk = 256 if rows % 256 == 0 else LANES
    tile = lambda i: jnp.minimum(i, n_tiles - 1)
    return pl.pallas_call(
        functools.partial(_dispatch_kernel, chunk=chunk),
        grid=(n_tiles + 1,),
        in_specs=[pl.BlockSpec((TOP_K, ts), lambda i: (0, tile(i))),
                  pl.BlockSpec((ts, D), lambda i: (tile(i), 0))],
        out_specs=pl.BlockSpec((1, rows, D), lambda i: (i, 0, 0)),
        out_shape=jax.ShapeDtypeStruct((n_tiles + 1, rows, D), F32),
        compiler_params=_cparams("arbitrary"),
        name="dispatch",
    )(pos, h2)


def _expert_kernel(be_ref, nu_ref, nx_ref, par_ref, cur_ref, nxt_ref, b1_ref, b2_ref, st_hbm, w1_hbm, w2_hbm, y_ref,
                   x_ref, w1f_ref, w2f_ref, w1b_ref, w2b_ref, sem, xsem, *, layer):
    i = pl.program_id(0)
    live = i < nu_ref[0]
    xslot = i % 2
    tm = x_ref.shape[1]

    def row_gather(table_ref, sl):
        return [pltpu.make_async_copy(st_hbm.at[pl.ds(pl.multiple_of(table_ref[0, 0, u] * SUB, SUB), SUB)],
                                      x_ref.at[sl, pl.ds(u * SUB, SUB)], xsem.at[sl])
                for u in range(tm // SUB)]

    def wait_rows(sl):
        pltpu.make_async_copy(st_hbm.at[pl.ds(0, tm)], x_ref.at[sl], xsem.at[sl]).wait()

    @pl.when(i == 0)
    def _():
        for cp in row_gather(cur_ref, 0):
            cp.start()
    e = be_ref[i]
    fresh = jnp.logical_or(i == 0, e != be_ref[jnp.maximum(i - 1, 0)])
    slot = par_ref[i]

    def weight_copies(expert, sl):
        return (pltpu.make_async_copy(w1_hbm.at[layer, expert], w1f_ref.at[sl], sem.at[0, sl]),
                pltpu.make_async_copy(w2_hbm.at[layer, expert], w2f_ref.at[sl], sem.at[1, sl]))

    @pl.when(i == 0)
    def _():
        for cp in weight_copies(e, slot):
            cp.start()

    @pl.when(jnp.logical_and(live, fresh))
    def _():
        for cp in weight_copies(e, slot):
            cp.wait()
        w1b_ref[...] = w1f_ref[slot].astype(BF16)
        w2b_ref[...] = w2f_ref[slot].astype(BF16)
        nxt = nx_ref[i]

        @pl.when(nxt >= 0)
        def _():
            for cp in weight_copies(nxt, 1 - slot):
                cp.start()

    @pl.when(live)
    def _():
        for cp in row_gather(nxt_ref, 1 - xslot):
            cp.start()
        wait_rows(xslot)
        dx = w2b_ref.shape[0]
        hu = jnp.dot(x_ref[xslot].astype(BF16), w1b_ref[...], preferred_element_type=F32) + b1_ref[0, 0]
        x_glu = jnp.minimum(hu[:, :dx], SWIGLU_LIMIT)
        x_lin = jnp.clip(hu[:, dx:], -SWIGLU_LIMIT, SWIGLU_LIMIT)
        act = (x_lin + 1.0) * (x_glu * jax.nn.sigmoid(SWIGLU_ALPHA * x_glu))
        y_ref[...] = jnp.dot(act.astype(BF16), w2b_ref[...], preferred_element_type=F32) + b2_ref[0, 0]

        @pl.when(i == nu_ref[0] - 1)
        def _():
            wait_rows(1 - xslot)

    @pl.when(jnp.logical_not(live))
    def _():
        y_ref[...] = jnp.zeros_like(y_ref)


def _experts(block_e, n_used, next_e, parity, units, staged, w1, b1, w2, b2, layer, tm):
    n_blocks = units.shape[0]
    D = staged.shape[1]
    P = n_blocks * tm
    _, E, _, H = w1.shape
    dx = w2.shape[2]
    n_units = tm // SUB
    exp4 = lambda i, be, nu, nx, par: (layer, be[i], 0, 0)
    cur = lambda i, be, nu, nx, par: (i, 0, 0)
    nxt = lambda i, be, nu, nx, par: (jnp.minimum(i + 1, n_blocks - 1), 0, 0)
    return pl.pallas_call(
        functools.partial(_expert_kernel, layer=layer),
        grid_spec=pltpu.PrefetchScalarGridSpec(
            num_scalar_prefetch=4,
            grid=(P // tm,),
            in_specs=[pl.BlockSpec((1, 1, n_units), cur, memory_space=pltpu.SMEM),
                      pl.BlockSpec((1, 1, n_units), nxt, memory_space=pltpu.SMEM),
                      pl.BlockSpec((1, 1, 1, H), exp4),
                      pl.BlockSpec((1, 1, 1, D), exp4),
                      pl.BlockSpec(memory_space=pl.ANY),
                      pl.BlockSpec(memory_space=pl.ANY),
                      pl.BlockSpec(memory_space=pl.ANY)],
            out_specs=pl.BlockSpec((tm, D), lambda i, *_: (i, 0)),
            scratch_shapes=[pltpu.VMEM((2, tm, D), F32),
                            pltpu.VMEM((2, D, H), F32), pltpu.VMEM((2, dx, D), F32),
                            pltpu.VMEM((D, H), BF16), pltpu.VMEM((dx, D), BF16),
                            pltpu.SemaphoreType.DMA((2, 2)), pltpu.SemaphoreType.DMA((2,))]),
        out_shape=jax.ShapeDtypeStruct((P, D), F32),
        compiler_params=_cparams("arbitrary"),
        name="experts",
    )(block_e, n_used, next_e, parity, units, units,
      b1.reshape(b1.shape[0], E, 1, H), b2.reshape(b2.shape[0], E, 1, D), staged, w1, w2)


def _combine_kernel(dst8_ref, n8_ref, src8_ref, units_ref,
                    x_ref, mod_ref, gp_ref, gfin_ref, ys_ref, o_ref, stage_ref, sem, *, final, chunk):
    j = pl.program_id(0) * pl.num_programs(1) + pl.program_id(1)
    n_tiles = pl.num_programs(0) * pl.num_programs(1)
    slot = j % 2
    ts = x_ref.shape[1]
    rows = stage_ref.shape[1]
    n_exp = (rows - ts * TOP_K) // SUB

    def seg_copy(sl):
        def build(src, dst, nrows):
            return pltpu.make_async_copy(ys_ref.at[pl.ds(dst, nrows)], stage_ref.at[sl, pl.ds(src, nrows)],
                                         sem.at[sl])
        return build

    def start_fetch(tile, sl):
        stage_ref[sl, ts * TOP_K:rows, :] = jnp.zeros((rows - ts * TOP_K, stage_ref.shape[2]), F32)
        _start_segment_copies(n8_ref, src8_ref, dst8_ref, tile * n_exp, n_exp, ts // SUB, seg_copy(sl))

    @pl.when(j == 0)
    def _():
        start_fetch(0, 0)

    @pl.when(j + 1 < n_tiles)
    def _():
        start_fetch(j + 1, 1 - slot)

    gp = gp_ref[...]
    gates = [jnp.broadcast_to(gp[:, kk:kk + 1], (ts, chunk)) for kk in range(TOP_K)]
    slots = [jnp.broadcast_to(gp[:, TOP_K + kk:TOP_K + kk + 1].astype(I32), (ts, chunk)) for kk in range(TOP_K)]
    _wait_segment_copies(units_ref[j], ts // SUB, seg_copy(slot))

    moe = jnp.zeros((ts, stage_ref.shape[2]), F32)
    for r0 in range(0, rows, chunk):
        rio = lax.broadcasted_iota(I32, (ts, chunk), 1) + r0
        wgt = jnp.zeros((ts, chunk), F32)
        for kk in range(TOP_K):
            wgt = jnp.where(rio == slots[kk], gates[kk], wgt)
        moe = moe + jnp.dot(wgt.astype(BF16), stage_ref[slot, r0:r0 + chunk, :].astype(BF16),
                            preferred_element_type=F32)
    gt2 = mod_ref[0][5:6]
    out = x_ref[0] + gt2 * moe
    if final:
        ms = jnp.mean(out * out, axis=-1, keepdims=True)
        out = out * lax.rsqrt(ms + EPS) * gfin_ref[...]
    o_ref[0] = out


def _combine(dst8, n8, src8, units, x1, mod, layer, gp, g_final, ys, n_exp, ts, final):
    B, S, D = x1.shape
    nts = S // ts
    rows = _stage_rows(ts, n_exp)
    chunk = 256 if rows % 256 == 0 else LANES
    return pl.pallas_call(
        functools.partial(_combine_kernel, final=final, chunk=chunk),
        grid_spec=pltpu.PrefetchScalarGridSpec(
            num_scalar_prefetch=4,
            grid=(B, nts),
            in_specs=[pl.BlockSpec((1, ts, D), lambda b, s, *_: (b, s, 0)),
                      pl.BlockSpec((None, 1, 6, D), lambda b, s, *_: (layer, b, 0, 0)),
                      pl.BlockSpec((ts, LANES), lambda b, s, *_: (b * nts + s, 0)),
                      pl.BlockSpec((1, D), lambda b, s, *_: (0, 0)),
                      pl.BlockSpec(memory_space=pl.ANY)],
            out_specs=pl.BlockSpec((1, ts, D), lambda b, s, *_: (b, s, 0)),
            scratch_shapes=[pltpu.VMEM((2, rows, D), F32), pltpu.SemaphoreType.DMA((2,))]),
        out_shape=jax.ShapeDtypeStruct((B, S, D), F32),
        compiler_params=_cparams("arbitrary", "arbitrary"),
        name="combine",
    )(dst8, n8, src8, units, x1, mod, gp, g_final.reshape(1, D), ys)


def _rope_tables(seq_len):
    freqs = ROPE_THETA ** (-jnp.arange(ROPE_FREQS, dtype=F32) / ROPE_FREQS)
    pos = jnp.arange(seq_len, dtype=I32)
    row_ang = (pos // GRID_W).astype(F32)[:, None] * freqs
    col_ang = (pos % GRID_W).astype(F32)[:, None] * freqs
    cos = jnp.concatenate([jnp.cos(row_ang)] * 2 + [jnp.cos(col_ang)] * 2, axis=1)
    sin = jnp.concatenate([-jnp.sin(row_ang), jnp.sin(row_ang), -jnp.sin(col_ang), jnp.sin(col_ang)], axis=1)
    reps = LANES // HEAD_DIM
    return jnp.tile(cos, (1, reps)), jnp.tile(sin, (1, reps))


def _tile_rows(n, prefs):
    for t in prefs:
        if n % t == 0:
            return t
    return n


def kernel(x, c, w_mod, b_mod, g_mix, w_in, g_q, g_k, w_dw, b_dw, g_cn, b_cn,
           w_out, g_ffn, w_router, b_router, w1, b1, w2, b2, g_final):
    B, S, D = x.shape
    L = w_mod.shape[0]
    E = w_router.shape[2]
    N = B * S
    ts = _tile_rows(S, (512, 256, 128))
    tq = _tile_rows(S, (256, 128))
    tm = 512
    n_tiles = N // ts
    max_rows = N * TOP_K + n_tiles * E * (SUB - 1)
    n_rows = (max_rows + tm - 1) // tm * tm + E * tm
    n_blocks = n_rows // tm

    mod = _modulation(c, w_mod, b_mod).reshape(L, B, 6, D)
    cos_t, sin_t = _rope_tables(S)
    head = jnp.arange(ATTN_WIDTH, dtype=I32) // HEAD_DIM
    bd = jnp.where(head[:, None] == head[None, :], 1.0 / HEAD_DIM, 0.0).astype(BF16)
    tok = jnp.arange(ts, dtype=I32)
    tri = (tok[:, None] < tok[None, :]).astype(BF16)
    eid = jnp.arange(E, dtype=I32)
    etri = (eid[None, :] < eid[:, None]).astype(BF16)

    for l in range(L):
        q, k, v, u = _inproj(x, mod, l, g_mix[l], w_in, g_q[l], g_k[l], bd, cos_t, sin_t, ts)
        attn = _attention(q, k, v, tq)
        conv = _conv(u, w_dw[l], b_dw[l], g_cn[l], b_cn[l])
        x1, h2, pos, gp, a8, n8, b8, tot = _outproj_router(
            x, attn, conv, w_out, mod, l, g_ffn[l], w_router[l], b_router[l], tri, etri, ts)

        rows_e = tot[:, 0].astype(I32) * SUB
        padded = (rows_e + tm - 1) // tm * tm
        pend = jnp.cumsum(padded)
        pstart = pend - padded
        n_used = (pend[-1] // tm).astype(I32)
        blk = jnp.minimum(jnp.arange(n_blocks, dtype=I32), n_used - 1) * tm
        block_e = jnp.minimum(jnp.sum(blk[:, None] >= pend[None, :], axis=1), E - 1).astype(I32)
        dst8 = (a8[:, :, 0] + (pstart // SUB)[None, :]).reshape(-1)
        n8f = n8[:, :, 0].reshape(-1)
        src8 = b8[:, :, 0].reshape(-1)
        units = jnp.sum(n8[:, :, 0], axis=1)
        tail0 = (pstart + rows_e) // SUB
        tailn = (padded - rows_e) // SUB

        has = padded > 0
        ordinal = jnp.cumsum(has.astype(I32)) - 1
        later = jnp.where(jnp.logical_and(has[None, :], eid[None, :] > eid[:, None]), eid[None, :], E)
        nxt = jnp.min(later, axis=1)
        next_e = jnp.where(nxt < E, nxt, -1).astype(I32)[block_e]
        parity = (ordinal % 2).astype(I32)[block_e]

        n_used = n_used.reshape(1)
        upt = _stage_rows(ts, E) // SUB
        dunit = jnp.arange(n_blocks * (tm // SUB), dtype=I32)
        ue = block_e[dunit // (tm // SUB)]
        local = dunit - (pstart // SUB)[ue]
        real = jnp.logical_and(local < (rows_e // SUB)[ue], dunit < n_used[0] * (tm // SUB))
        a8t = a8[:, :, 0].T[ue]
        tile_of = jnp.sum(a8t <= local[:, None], axis=1).astype(I32) - 1
        within = local - jnp.take_along_axis(a8t, tile_of[:, None], axis=1)[:, 0]
        src_unit = tile_of * upt + b8[:, :, 0][tile_of, ue] + within
        row_units = jnp.where(real, src_unit, n_tiles * upt).astype(I32).reshape(n_blocks, 1, tm // SUB)

        staged = _dispatch(pos, h2, E, ts).reshape((n_tiles + 1) * upt * SUB, D)
        ys = _experts(block_e, n_used, next_e, parity, row_units, staged, w1, b1, w2, b2, l, tm)
        x = _combine(dst8, n8f, src8, units, x1, mod, l, gp, g_final, ys, E, ts, final=(l == L - 1))
    return x
```

```python
import functools

import jax
import jax.numpy as jnp
from jax import lax
from jax.experimental import pallas as pl
from jax.experimental.pallas import tpu as pltpu

F32 = jnp.float32
BF16 = jnp.bfloat16
I32 = jnp.int32

HEAD_DIM = 64
N_Q_HEADS = 8
N_KV_HEADS = 2
ATTN_WIDTH = N_Q_HEADS * HEAD_DIM
KV_WIDTH = N_KV_HEADS * HEAD_DIM
GRID_W = 64
ROPE_THETA = 10000.0
ROPE_FREQS = HEAD_DIM // 4
CONV_WIDTH = 31
CONV_PAD = CONV_WIDTH // 2
TOP_K = 4
SWIGLU_LIMIT = 7.0
SWIGLU_ALPHA = 1.702
EPS = 1e-6
LOG2E = 1.4426950408889634

LANES = 128
SUB = 8
VMEM_LIMIT = 56 * 1024 * 1024


def _cparams(*sem):
    return pltpu.CompilerParams(dimension_semantics=sem, vmem_limit_bytes=VMEM_LIMIT)


def _stage_rows(ts, n_exp):
    return ts * TOP_K + n_exp * SUB


def _mod_kernel(c_ref, w_ref, b_ref, o_ref):
    c = c_ref[...]
    c_act = (c * jax.nn.sigmoid(c)).astype(BF16)
    o_ref[0] = jnp.dot(c_act, w_ref[0].astype(BF16), preferred_element_type=F32) + b_ref[0]


def _modulation(c, w_mod, b_mod):
    L, D, W = w_mod.shape
    B = c.shape[0]
    tn = 1536
    return pl.pallas_call(
        _mod_kernel,
        grid=(L, W // tn),
        in_specs=[pl.BlockSpec((B, D), lambda l, j: (0, 0)),
                  pl.BlockSpec((1, D, tn), lambda l, j: (l, 0, j)),
                  pl.BlockSpec((1, 1, tn), lambda l, j: (l, 0, j))],
        out_specs=pl.BlockSpec((1, B, tn), lambda l, j: (l, 0, j)),
        out_shape=jax.ShapeDtypeStruct((L, B, W), F32),
        compiler_params=_cparams("parallel", "parallel"),
        name="modulation",
    )(c, w_mod, b_mod.reshape(L, 1, W))


def _head_rmsnorm(t, gain, bd):
    msq = jnp.dot((t * t).astype(BF16), bd, preferred_element_type=F32)
    return t * lax.rsqrt(msq + EPS) * gain


def _rope(t, cos, sin):
    w = t.shape[1]
    reps = w // LANES
    cosw = jnp.concatenate([cos] * reps, axis=1) if reps > 1 else cos
    sinw = jnp.concatenate([sin] * reps, axis=1) if reps > 1 else sin
    lane = lax.broadcasted_iota(I32, t.shape, 1)
    first = (lane & ROPE_FREQS) == 0
    partner = jnp.where(first, pltpu.roll(t, w - ROPE_FREQS, 1), pltpu.roll(t, ROPE_FREQS, 1))
    return t * cosw + partner * sinw


def _inproj_kernel(x_ref, mod_ref, gmix_ref, w_ref, gq_ref, gk_ref, bd_ref, cos_ref, sin_ref,
                   q_ref, k_ref, v_ref, u_ref, wb_ref):
    @pl.when(jnp.logical_and(pl.program_id(0) == 0, pl.program_id(1) == 0))
    def _():
        wb_ref[...] = w_ref[0].astype(BF16)

    x = x_ref[0]
    mod = mod_ref[0]
    sh1, sc1 = mod[0:1], mod[1:2]
    ms = jnp.mean(x * x, axis=-1, keepdims=True)
    xn = x * lax.rsqrt(ms + EPS) * gmix_ref[...]
    h = (xn * (1.0 + sc1) + sh1).astype(BF16)
    proj = jnp.dot(h, wb_ref[...], preferred_element_type=F32)
    a0, a1, a2 = ATTN_WIDTH, ATTN_WIDTH + KV_WIDTH, ATTN_WIDTH + 2 * KV_WIDTH
    cw = (proj.shape[1] - a2) // 2
    q, k, v = proj[:, :a0], proj[:, a0:a1], proj[:, a1:a2]
    ca, cg = proj[:, a2:a2 + cw], proj[:, a2 + cw:]
    cos, sin = cos_ref[...], sin_ref[...]
    bd = bd_ref[...]
    q = _rope(_head_rmsnorm(q, gq_ref[...], bd), cos, sin)
    k = _rope(_head_rmsnorm(k, gk_ref[...], bd[:KV_WIDTH, :KV_WIDTH]), cos, sin)
    q_ref[0] = (q * (HEAD_DIM ** -0.5 * LOG2E)).astype(BF16)
    k_ref[0] = k.astype(BF16)
    v_ref[0] = v.astype(BF16)
    u_ref[0] = ca * jax.nn.sigmoid(cg)


def _inproj(x, mod, layer, g_mix, w_in, g_q, g_k, bd, cos_t, sin_t, ts):
    B, S, D = x.shape
    W = w_in.shape[2]
    cw = (W - ATTN_WIDTH - 2 * KV_WIDTH) // 2
    gq = jnp.tile(g_q, N_Q_HEADS).reshape(1, ATTN_WIDTH)
    gk = jnp.tile(g_k, N_KV_HEADS).reshape(1, KV_WIDTH)
    const = lambda b, s: (0, 0)
    return pl.pallas_call(
        _inproj_kernel,
        grid=(B, S // ts),
        in_specs=[pl.BlockSpec((1, ts, D), lambda b, s: (b, s, 0)),
                  pl.BlockSpec((None, 1, 6, D), lambda b, s: (layer, b, 0, 0)),
                  pl.BlockSpec((1, D), const),
                  pl.BlockSpec((1, D, W), lambda b, s: (layer, 0, 0)),
                  pl.BlockSpec((1, ATTN_WIDTH), const),
                  pl.BlockSpec((1, KV_WIDTH), const),
                  pl.BlockSpec((ATTN_WIDTH, ATTN_WIDTH), const),
                  pl.BlockSpec((ts, LANES), lambda b, s: (s, 0)),
                  pl.BlockSpec((ts, LANES), lambda b, s: (s, 0))],
        out_specs=[pl.BlockSpec((1, ts, ATTN_WIDTH), lambda b, s: (b, s, 0)),
                   pl.BlockSpec((1, ts, KV_WIDTH), lambda b, s: (b, s, 0)),
                   pl.BlockSpec((1, ts, KV_WIDTH), lambda b, s: (b, s, 0)),
                   pl.BlockSpec((1, ts, cw), lambda b, s: (b, s, 0))],
        out_shape=[jax.ShapeDtypeStruct((B, S, ATTN_WIDTH), BF16),
                   jax.ShapeDtypeStruct((B, S, KV_WIDTH), BF16),
                   jax.ShapeDtypeStruct((B, S, KV_WIDTH), BF16),
                   jax.ShapeDtypeStruct((B, S, cw), F32)],
        scratch_shapes=[pltpu.VMEM((D, W), BF16)],
        compiler_params=_cparams("arbitrary", "arbitrary"),
        name="inproj",
    )(x, mod, g_mix.reshape(1, D), w_in, gq, gk, bd, cos_t, sin_t)


def _attn_kernel(q_ref, k_ref, v_ref, o_ref):
    k = k_ref[0]
    v = v_ref[0]
    tq = q_ref.shape[1]
    lane = lax.broadcasted_iota(I32, (tq, LANES), 1)
    low = lane < HEAD_DIM
    heads_per_tile = LANES // HEAD_DIM
    group = N_Q_HEADS // N_KV_HEADS

    def scores(h):
        j, hh = divmod(h, heads_per_tile)
        g = h // group
        qt = q_ref[0, :, j * LANES:(j + 1) * LANES].astype(F32)
        qm = jnp.where(low if hh == 0 else jnp.logical_not(low), qt, 0.0)
        if hh != g:
            qm = pltpu.roll(qm, HEAD_DIM, 1)
        return lax.dot_general(qm.astype(BF16), k, (((1,), (1,)), ((), ())),
                               preferred_element_type=F32)

    s_next = scores(0)
    placed = []
    for h in range(N_Q_HEADS):
        s = s_next
        if h + 1 < N_Q_HEADS:
            s_next = scores(h + 1)
        j, hh = divmod(h, heads_per_tile)
        m = jnp.max(s, axis=-1, keepdims=True)
        p = jnp.exp2(s - m)
        l = jnp.sum(p, axis=-1, keepdims=True)
        o = jnp.dot(p.astype(BF16), v, preferred_element_type=F32) / l
        if hh != h // group:
            o = pltpu.roll(o, HEAD_DIM, 1)
        placed.append(o)
        if hh == heads_per_tile - 1:
            o_ref[0, :, j * LANES:(j + 1) * LANES] = jnp.where(low, placed[0], placed[1]).astype(BF16)
            placed = []


def _attention(q, k, v, tq):
    B, S, _ = q.shape
    return pl.pallas_call(
        _attn_kernel,
        grid=(B, S // tq),
        in_specs=[pl.BlockSpec((1, tq, ATTN_WIDTH), lambda b, i: (b, i, 0)),
                  pl.BlockSpec((1, S, KV_WIDTH), lambda b, i: (b, 0, 0)),
                  pl.BlockSpec((1, S, KV_WIDTH), lambda b, i: (b, 0, 0))],
        out_specs=pl.BlockSpec((1, tq, ATTN_WIDTH), lambda b, i: (b, i, 0)),
        out_shape=jax.ShapeDtypeStruct((B, S, ATTN_WIDTH), BF16),
        compiler_params=_cparams("parallel", "parallel"),
        name="attention",
    )(q, k, v)


def _conv_kernel(u_ref, w_ref, b_ref, g_ref, beta_ref, o_ref, pad_ref, *, tc):
    S, C = u_ref.shape[1], u_ref.shape[2]
    halo = 2 * SUB
    pad_ref[0:halo, :] = jnp.zeros((halo, C), F32)
    pad_ref[halo + S:halo + S + halo, :] = jnp.zeros((halo, C), F32)
    pad_ref[halo:halo + S, :] = u_ref[0]
    span = tc + 2 * halo

    def tile(i, carry):
        s0 = pl.multiple_of(i * tc, tc)
        cols = []
        for c0 in range(0, C, LANES):
            win = pad_ref[pl.ds(s0, span), c0:c0 + LANES]
            acc = jnp.zeros((tc, LANES), F32)
            for r in range(SUB):
                shifted = win if r == 0 else pltpu.roll(win, span - r, 0)
                for j in range(CONV_WIDTH):
                    off = halo - CONV_PAD + j
                    if off % SUB == r:
                        a = off - r
                        acc = acc + shifted[a:a + tc, :] * w_ref[j:j + 1, c0:c0 + LANES]
            cols.append(acc)
        acc = jnp.concatenate(cols, axis=1) + b_ref[...]
        mu = jnp.mean(acc, axis=-1, keepdims=True)
        d = acc - mu
        var = jnp.mean(d * d, axis=-1, keepdims=True)
        y = d * lax.rsqrt(var + EPS) * g_ref[...] + beta_ref[...]
        o_ref[0, pl.ds(s0, tc), :] = (y * jax.nn.sigmoid(y)).astype(BF16)
        return carry

    lax.fori_loop(0, S // tc, tile, 0)


def _conv(u, w_dw, b_dw, g_cn, b_cn):
    B, S, C = u.shape
    tc = _tile_rows(S, (128, 64))
    const = lambda b: (0, 0)
    return pl.pallas_call(
        functools.partial(_conv_kernel, tc=tc),
        grid=(B,),
        in_specs=[pl.BlockSpec((1, S, C), lambda b: (b, 0, 0)),
                  pl.BlockSpec((CONV_WIDTH, C), const),
                  pl.BlockSpec((1, C), const),
                  pl.BlockSpec((1, C), const),
                  pl.BlockSpec((1, C), const)],
        out_specs=pl.BlockSpec((1, S, C), lambda b: (b, 0, 0)),
        out_shape=jax.ShapeDtypeStruct((B, S, C), BF16),
        scratch_shapes=[pltpu.VMEM((S + 4 * SUB, C), F32)],
        compiler_params=_cparams("parallel"),
        name="conv",
    )(u, w_dw, b_dw.reshape(1, C), g_cn.reshape(1, C), b_cn.reshape(1, C))


def _split_bf16(a):
    hi = a.astype(BF16)
    lo = (a - hi.astype(F32)).astype(BF16)
    return hi, lo


def _outproj_router_kernel(x_ref, attn_ref, conv_ref, wout_ref, mod_ref, gffn_ref, wr_ref, br_ref, tri_ref, etri_ref,
                           x1_ref, h2_ref, pos_ref, gp_ref, a8_ref, n8_ref, b8_ref, tot_ref, carry_ref, wb_ref):
    first_step = jnp.logical_and(pl.program_id(0) == 0, pl.program_id(1) == 0)

    @pl.when(first_step)
    def _():
        carry_ref[...] = jnp.zeros_like(carry_ref)
        wb_ref[...] = wout_ref[0].astype(BF16)

    mod = mod_ref[0]
    gt1, sh2, sc2 = mod[2:3], mod[3:4], mod[4:5]
    aw = attn_ref.shape[2]
    mix = (jnp.dot(attn_ref[0], wb_ref[:aw, :], preferred_element_type=F32)
           + jnp.dot(conv_ref[0], wb_ref[aw:, :], preferred_element_type=F32))
    x1 = x_ref[0] + gt1 * mix
    x1_ref[0] = x1
    ms = jnp.mean(x1 * x1, axis=-1, keepdims=True)
    h2 = x1 * lax.rsqrt(ms + EPS) * gffn_ref[...] * (1.0 + sc2) + sh2
    h2_ref[...] = h2.astype(BF16)

    h_hi, h_lo = _split_bf16(h2)
    w_hi, w_lo = _split_bf16(wr_ref[...])
    nt = (((1,), (1,)), ((), ()))
    logits = (lax.dot_general(w_hi, h_hi, nt, preferred_element_type=F32)
              + lax.dot_general(w_lo, h_hi, nt, preferred_element_type=F32)
              + lax.dot_general(w_hi, h_lo, nt, preferred_element_type=F32)) + br_ref[...]
    n_exp, ts = logits.shape
    reps = ts // LANES
    eio = lax.broadcasted_iota(I32, logits.shape, 0).astype(F32)
    vals, hits = [], []
    work = logits
    for _ in range(TOP_K):
        m = jnp.max(work, axis=0, keepdims=True)
        ik = jnp.min(jnp.where(work == m, eio, float(n_exp)), axis=0, keepdims=True)
        hit = eio == ik
        work = jnp.where(hit, -jnp.inf, work)
        vals.append(m)
        hits.append(hit)
    ex = [jnp.exp(v - vals[0]) for v in vals]
    den = ex[0] + ex[1] + ex[2] + ex[3]

    chosen = jnp.zeros(logits.shape, F32)
    for hit in hits:
        chosen = chosen + hit.astype(F32)
    prefix = jnp.dot(chosen.astype(BF16), tri_ref[...], preferred_element_type=F32)
    cnt = jnp.sum(chosen, axis=1, keepdims=True)
    n8 = jnp.broadcast_to(jnp.floor((cnt + (SUB - 1.0)) * (1.0 / SUB)), (n_exp, LANES))
    b8 = jnp.dot(etri_ref[...], n8.astype(BF16), preferred_element_type=F32)
    slot = jnp.concatenate([b8 * float(SUB)] * reps, axis=1) + prefix
    pos = [jnp.sum(jnp.where(hit, slot, 0.0), axis=0, keepdims=True) for hit in hits]
    for kk in range(TOP_K):
        pos_ref[kk:kk + 1, :] = pos[kk].astype(I32)
    a8_ref[0] = carry_ref[...].astype(I32)
    n8_ref[0] = n8.astype(I32)
    b8_ref[0] = b8.astype(I32)
    carry_ref[...] = carry_ref[...] + n8
    tot_ref[...] = carry_ref[...]

    rows = [e / den for e in ex] + pos + [jnp.zeros((LANES - 2 * TOP_K, ts), F32)]
    gp_ref[...] = jnp.concatenate(rows, axis=0).T


def _outproj_router(x, attn, conv, w_out, mod, layer, g_ffn, w_router, b_router, tri, etri, ts):
    B, S, D = x.shape
    N = B * S
    E = w_router.shape[1]
    nts = S // ts
    aw, cw = attn.shape[2], conv.shape[2]
    const = lambda b, s: (0, 0)
    tok = lambda b, s: (0, b * nts + s)
    tile3 = lambda b, s: (b * nts + s, 0, 0)
    return pl.pallas_call(
        _outproj_router_kernel,
        grid=(B, nts),
        in_specs=[pl.BlockSpec((1, ts, D), lambda b, s: (b, s, 0)),
                  pl.BlockSpec((1, ts, aw), lambda b, s: (b, s, 0)),
                  pl.BlockSpec((1, ts, cw), lambda b, s: (b, s, 0)),
                  pl.BlockSpec((1, aw + cw, D), lambda b, s: (layer, 0, 0)),
                  pl.BlockSpec((None, 1, 6, D), lambda b, s: (layer, b, 0, 0)),
                  pl.BlockSpec((1, D), const),
                  pl.BlockSpec((E, D), const),
                  pl.BlockSpec((E, 1), const),
                  pl.BlockSpec((ts, ts), const),
                  pl.BlockSpec((E, E), const)],
        out_specs=[pl.BlockSpec((1, ts, D), lambda b, s: (b, s, 0)),
                   pl.BlockSpec((ts, D), lambda b, s: (b * nts + s, 0)),
                   pl.BlockSpec((TOP_K, ts), tok),
                   pl.BlockSpec((ts, LANES), lambda b, s: (b * nts + s, 0)),
                   pl.BlockSpec((1, E, LANES), tile3),
                   pl.BlockSpec((1, E, LANES), tile3),
                   pl.BlockSpec((1, E, LANES), tile3),
                   pl.BlockSpec((E, LANES), const)],
        out_shape=[jax.ShapeDtypeStruct((B, S, D), F32),
                   jax.ShapeDtypeStruct((N, D), BF16),
                   jax.ShapeDtypeStruct((TOP_K, N), I32),
                   jax.ShapeDtypeStruct((N, LANES), F32),
                   jax.ShapeDtypeStruct((B * nts, E, LANES), I32),
                   jax.ShapeDtypeStruct((B * nts, E, LANES), I32),
                   jax.ShapeDtypeStruct((B * nts, E, LANES), I32),
                   jax.ShapeDtypeStruct((E, LANES), F32)],
        scratch_shapes=[pltpu.VMEM((E, LANES), F32), pltpu.VMEM((aw + cw, D), BF16)],
        compiler_params=_cparams("arbitrary", "arbitrary"),
        name="outproj_router",
    )(x, attn, conv, w_out, mod, g_ffn.reshape(1, D), w_router.T, b_router.reshape(E, 1), tri, etri)


def _start_segment_copies(n8_ref, a_ref, b_ref, base, n_exp, max_units, copy):
    n_bits = max_units.bit_length()
    common = min(4, n_bits)
    for e in range(n_exp):
        n = n8_ref[base + e]
        a0 = a_ref[base + e]
        b0 = b_ref[base + e]

        def chunk(bit):
            done = (n >> (bit + 1)) << (bit + 1)

            @pl.when(((n >> bit) & 1) == 1)
            def _():
                copy(pl.multiple_of((a0 + done) * SUB, SUB), pl.multiple_of((b0 + done) * SUB, SUB),
                     SUB << bit).start()

        for bit in reversed(range(common)):
            chunk(bit)

        def rare(_, c):
            for bit in reversed(range(common, n_bits)):
                chunk(bit)
            return c

        lax.fori_loop(0, (n >> common != 0).astype(I32), rare, 0)


def _wait_segment_copies(units, big_units, copy):
    shift = big_units.bit_length() - 1
    lax.fori_loop(0, units >> shift, lambda u, c: (copy(0, 0, SUB * big_units).wait(), c)[1], 0)
    lax.fori_loop(0, units & (big_units - 1), lambda u, c: (copy(0, 0, SUB).wait(), c)[1], 0)


def _dispatch_kernel(dst8_ref, n8_ref, src8_ref, units_ref, tail0_ref, tailn_ref, nu_ref,
                     pos_ref, h_ref, xs_ref, stage_ref, zero_ref, sem, zsem, *, chunk):
    j = pl.program_id(0)
    last = pl.num_programs(0) - 1
    slot = j % 2
    ts = h_ref.shape[0]
    n_exp = tail0_ref.shape[0]
    rows = stage_ref.shape[1]
    h = h_ref[...]
    pos = pos_ref[...]
    for r0 in range(0, rows, chunk):
        rio = lax.broadcasted_iota(I32, (chunk, ts), 0) + r0
        onehot = jnp.zeros((chunk, ts), F32)
        for kk in range(TOP_K):
            onehot = jnp.where(rio == pos[kk:kk + 1, :], 1.0, onehot)
        stage_ref[slot, r0:r0 + chunk, :] = jnp.dot(onehot.astype(BF16), h, preferred_element_type=F32)

    def seg_copy(sl):
        def build(src, dst, nrows):
            return pltpu.make_async_copy(stage_ref.at[sl, pl.ds(src, nrows)], xs_ref.at[pl.ds(dst, nrows)],
                                         sem.at[sl])
        return build

    _start_segment_copies(n8_ref, src8_ref, dst8_ref, j * n_exp, n_exp, ts // SUB, seg_copy(slot))

    @pl.when(j > 0)
    def _():
        _wait_segment_copies(units_ref[j - 1], ts // SUB, seg_copy(1 - slot))

    @pl.when(j == last)
    def _():
        zero_ref[...] = jnp.zeros_like(zero_ref)

        def zero_copy(u):
            return pltpu.make_async_copy(zero_ref.at[pl.ds(0, SUB)],
                                         xs_ref.at[pl.ds(pl.multiple_of(u * SUB, SUB), SUB)], zsem)

        for e in range(n_exp):
            lo = tail0_ref[e]
            hi = lo + tailn_ref[e]
            lax.fori_loop(lo, hi, lambda u, c: (zero_copy(u).start(), c)[1], 0)
            lax.fori_loop(lo, hi, lambda u, c: (zero_copy(u).wait(), c)[1], 0)

        tm = zero_ref.shape[0]

        def zero_block(b):
            return pltpu.make_async_copy(zero_ref, xs_ref.at[pl.ds(pl.multiple_of(b * tm, tm), tm)], zsem)

        n_blocks = xs_ref.shape[0] // tm
        lax.fori_loop(nu_ref[0], n_blocks, lambda b, c: (zero_block(b).start(), c)[1], 0)
        lax.fori_loop(nu_ref[0], n_blocks, lambda b, c: (zero_block(b).wait(), c)[1], 0)
        _wait_segment_copies(units_ref[j], ts // SUB, seg_copy(slot))


def _dispatch(dst8, n8, src8, units, tail0, tailn, n_used, pos, h2, n_rows, ts, tm):
    N, D = h2.shape
    E = tail0.shape[0]
    rows = _stage_rows(ts, E)
    chunk = 256 if rows % 256 == 0 else LANES
    return pl.pallas_call(
        functools.partial(_dispatch_kernel, chunk=chunk),
        grid_spec=pltpu.PrefetchScalarGridSpec(
            num_scalar_prefetch=7,
            grid=(N // ts,),
            in_specs=[pl.BlockSpec((TOP_K, ts), lambda i, *_: (0, i)),
                      pl.BlockSpec((ts, D), lambda i, *_: (i, 0))],
            out_specs=pl.BlockSpec(memory_space=pl.ANY),
            scratch_shapes=[pltpu.VMEM((2, rows, D), F32),
                            pltpu.VMEM((tm, D), F32),
                            pltpu.SemaphoreType.DMA((2,)),
                            pltpu.SemaphoreType.DMA]),
        out_shape=jax.ShapeDtypeStruct((n_rows, D), F32),
        compiler_params=_cparams("arbitrary"),
        name="dispatch",
    )(dst8, n8, src8, units, tail0, tailn, n_used, pos, h2)


def _expert_kernel(be_ref, nu_ref, nx_ref, par_ref, x_ref, b1_ref, b2_ref, w1_hbm, w2_hbm, y_ref,
                   w1f_ref, w2f_ref, w1b_ref, w2b_ref, sem, *, layer):
    i = pl.program_id(0)
    live = i < nu_ref[0]
    e = be_ref[i]
    fresh = jnp.logical_or(i == 0, e != be_ref[jnp.maximum(i - 1, 0)])
    slot = par_ref[i]

    def weight_copies(expert, sl):
        return (pltpu.make_async_copy(w1_hbm.at[layer, expert], w1f_ref.at[sl], sem.at[0, sl]),
                pltpu.make_async_copy(w2_hbm.at[layer, expert], w2f_ref.at[sl], sem.at[1, sl]))

    @pl.when(i == 0)
    def _():
        for cp in weight_copies(e, slot):
            cp.start()

    @pl.when(jnp.logical_and(live, fresh))
    def _():
        for cp in weight_copies(e, slot):
            cp.wait()
        w1b_ref[...] = w1f_ref[slot].astype(BF16)
        w2b_ref[...] = w2f_ref[slot].astype(BF16)
        nxt = nx_ref[i]

        @pl.when(nxt >= 0)
        def _():
            for cp in weight_copies(nxt, 1 - slot):
                cp.start()

    @pl.when(live)
    def _():
        dx = w2b_ref.shape[0]
        hu = jnp.dot(x_ref[...].astype(BF16), w1b_ref[...], preferred_element_type=F32) + b1_ref[0, 0]
        x_glu = jnp.minimum(hu[:, :dx], SWIGLU_LIMIT)
        x_lin = jnp.clip(hu[:, dx:], -SWIGLU_LIMIT, SWIGLU_LIMIT)
        act = (x_lin + 1.0) * (x_glu * jax.nn.sigmoid(SWIGLU_ALPHA * x_glu))
        y_ref[...] = jnp.dot(act.astype(BF16), w2b_ref[...], preferred_element_type=F32) + b2_ref[0, 0]

    @pl.when(jnp.logical_not(live))
    def _():
        y_ref[...] = jnp.zeros_like(y_ref)


def _experts(block_e, n_used, next_e, parity, xs, w1, b1, w2, b2, layer, tm):
    P, D = xs.shape
    _, E, _, H = w1.shape
    dx = w2.shape[2]
    row = lambda i, be, nu, nx, par: (jnp.minimum(i, nu[0] - 1), 0)
    exp4 = lambda i, be, nu, nx, par: (layer, be[i], 0, 0)
    return pl.pallas_call(
        functools.partial(_expert_kernel, layer=layer),
        grid_spec=pltpu.PrefetchScalarGridSpec(
            num_scalar_prefetch=4,
            grid=(P // tm,),
            in_specs=[pl.BlockSpec((tm, D), row),
                      pl.BlockSpec((1, 1, 1, H), exp4),
                      pl.BlockSpec((1, 1, 1, D), exp4),
                      pl.BlockSpec(memory_space=pl.ANY),
                      pl.BlockSpec(memory_space=pl.ANY)],
            out_specs=pl.BlockSpec((tm, D), lambda i, *_: (i, 0)),
            scratch_shapes=[pltpu.VMEM((2, D, H), F32), pltpu.VMEM((2, dx, D), F32),
                            pltpu.VMEM((D, H), BF16), pltpu.VMEM((dx, D), BF16),
                            pltpu.SemaphoreType.DMA((2, 2))]),
        out_shape=jax.ShapeDtypeStruct((P, D), F32),
        compiler_params=_cparams("arbitrary"),
        name="experts",
    )(block_e, n_used, next_e, parity, xs,
      b1.reshape(b1.shape[0], E, 1, H), b2.reshape(b2.shape[0], E, 1, D), w1, w2)


def _combine_kernel(dst8_ref, n8_ref, src8_ref, units_ref,
                    x_ref, mod_ref, gp_ref, gfin_ref, ys_ref, o_ref, stage_ref, sem, *, final, chunk):
    j = pl.program_id(0) * pl.num_programs(1) + pl.program_id(1)
    n_tiles = pl.num_programs(0) * pl.num_programs(1)
    slot = j % 2
    ts = x_ref.shape[1]
    rows = stage_ref.shape[1]
    n_exp = (rows - ts * TOP_K) // SUB

    def seg_copy(sl):
        def build(src, dst, nrows):
            return pltpu.make_async_copy(ys_ref.at[pl.ds(dst, nrows)], stage_ref.at[sl, pl.ds(src, nrows)],
                                         sem.at[sl])
        return build

    def start_fetch(tile, sl):
        stage_ref[sl, ts * TOP_K:rows, :] = jnp.zeros((rows - ts * TOP_K, stage_ref.shape[2]), F32)
        _start_segment_copies(n8_ref, src8_ref, dst8_ref, tile * n_exp, n_exp, ts // SUB, seg_copy(sl))

    @pl.when(j == 0)
    def _():
        start_fetch(0, 0)

    @pl.when(j + 1 < n_tiles)
    def _():
        start_fetch(j + 1, 1 - slot)

    gp = gp_ref[...]
    gates = [jnp.broadcast_to(gp[:, kk:kk + 1], (ts, chunk)) for kk in range(TOP_K)]
    slots = [jnp.broadcast_to(gp[:, TOP_K + kk:TOP_K + kk + 1].astype(I32), (ts, chunk)) for kk in range(TOP_K)]
    _wait_segment_copies(units_ref[j], ts // SUB, seg_copy(slot))

    moe = jnp.zeros((ts, stage_ref.shape[2]), F32)
    for r0 in range(0, rows, chunk):
        rio = lax.broadcasted_iota(I32, (ts, chunk), 1) + r0
        wgt = jnp.zeros((ts, chunk), F32)
        for kk in range(TOP_K):
            wgt = jnp.where(rio == slots[kk], gates[kk], wgt)
        moe = moe + jnp.dot(wgt.astype(BF16), stage_ref[slot, r0:r0 + chunk, :].astype(BF16),
                            preferred_element_type=F32)
    gt2 = mod_ref[0][5:6]
    out = x_ref[0] + gt2 * moe
    if final:
        ms = jnp.mean(out * out, axis=-1, keepdims=True)
        out = out * lax.rsqrt(ms + EPS) * gfin_ref[...]
    o_ref[0] = out


def _combine(dst8, n8, src8, units, x1, mod, layer, gp, g_final, ys, n_exp, ts, final):
    B, S, D = x1.shape
    nts = S // ts
    rows = _stage_rows(ts, n_exp)
    chunk = 256 if rows % 256 == 0 else LANES
    return pl.pallas_call(
        functools.partial(_combine_kernel, final=final, chunk=chunk),
        grid_spec=pltpu.PrefetchScalarGridSpec(
            num_scalar_prefetch=4,
            grid=(B, nts),
            in_specs=[pl.BlockSpec((1, ts, D), lambda b, s, *_: (b, s, 0)),
                      pl.BlockSpec((None, 1, 6, D), lambda b, s, *_: (layer, b, 0, 0)),
                      pl.BlockSpec((ts, LANES), lambda b, s, *_: (b * nts + s, 0)),
                      pl.BlockSpec((1, D), lambda b, s, *_: (0, 0)),
                      pl.BlockSpec(memory_space=pl.ANY)],
            out_specs=pl.BlockSpec((1, ts, D), lambda b, s, *_: (b, s, 0)),
            scratch_shapes=[pltpu.VMEM((2, rows, D), F32), pltpu.SemaphoreType.DMA((2,))]),
        out_shape=jax.ShapeDtypeStruct((B, S, D), F32),
        compiler_params=_cparams("arbitrary", "arbitrary"),
        name="combine",
    )(dst8, n8, src8, units, x1, mod, gp, g_final.reshape(1, D), ys)


def _rope_tables(seq_len):
    freqs = ROPE_THETA ** (-jnp.arange(ROPE_FREQS, dtype=F32) / ROPE_FREQS)
    pos = jnp.arange(seq_len, dtype=I32)
    row_ang = (pos // GRID_W).astype(F32)[:, None] * freqs
    col_ang = (pos % GRID_W).astype(F32)[:, None] * freqs
    cos = jnp.concatenate([jnp.cos(row_ang)] * 2 + [jnp.cos(col_ang)] * 2, axis=1)
    sin = jnp.concatenate([-jnp.sin(row_ang), jnp.sin(row_ang), -jnp.sin(col_ang), jnp.sin(col_ang)], axis=1)
    reps = LANES // HEAD_DIM
    return jnp.tile(cos, (1, reps)), jnp.tile(sin, (1, reps))


def _tile_rows(n, prefs):
    for t in prefs:
        if n % t == 0:
            return t
    return n


def kernel(x, c, w_mod, b_mod, g_mix, w_in, g_q, g_k, w_dw, b_dw, g_cn, b_cn,
           w_out, g_ffn, w_router, b_router, w1, b1, w2, b2, g_final):
    B, S, D = x.shape
    L = w_mod.shape[0]
    E = w_router.shape[2]
    N = B * S
    ts = _tile_rows(S, (512, 256, 128))
    tq = _tile_rows(S, (256, 128))
    tm = 512
    n_tiles = N // ts
    max_rows = N * TOP_K + n_tiles * E * (SUB - 1)
    n_rows = (max_rows + tm - 1) // tm * tm + E * tm
    n_blocks = n_rows // tm

    mod = _modulation(c, w_mod, b_mod).reshape(L, B, 6, D)
    cos_t, sin_t = _rope_tables(S)
    head = jnp.arange(ATTN_WIDTH, dtype=I32) // HEAD_DIM
    bd = jnp.where(head[:, None] == head[None, :], 1.0 / HEAD_DIM, 0.0).astype(BF16)
    tok = jnp.arange(ts, dtype=I32)
    tri = (tok[:, None] < tok[None, :]).astype(BF16)
    eid = jnp.arange(E, dtype=I32)
    etri = (eid[None, :] < eid[:, None]).astype(BF16)

    for l in range(L):
        q, k, v, u = _inproj(x, mod, l, g_mix[l], w_in, g_q[l], g_k[l], bd, cos_t, sin_t, ts)
        attn = _attention(q, k, v, tq)
        conv = _conv(u, w_dw[l], b_dw[l], g_cn[l], b_cn[l])
        x1, h2, pos, gp, a8, n8, b8, tot = _outproj_router(
            x, attn, conv, w_out, mod, l, g_ffn[l], w_router[l], b_router[l], tri, etri, ts)

        rows_e = tot[:, 0].astype(I32) * SUB
        padded = (rows_e + tm - 1) // tm * tm
        pend = jnp.cumsum(padded)
        pstart = pend - padded
        n_used = (pend[-1] // tm).astype(I32)
        blk = jnp.minimum(jnp.arange(n_blocks, dtype=I32), n_used - 1) * tm
        block_e = jnp.minimum(jnp.sum(blk[:, None] >= pend[None, :], axis=1), E - 1).astype(I32)
        dst8 = (a8[:, :, 0] + (pstart // SUB)[None, :]).reshape(-1)
        n8f = n8[:, :, 0].reshape(-1)
        src8 = b8[:, :, 0].reshape(-1)
        units = jnp.sum(n8[:, :, 0], axis=1)
        tail0 = (pstart + rows_e) // SUB
        tailn = (padded - rows_e) // SUB

        has = padded > 0
        ordinal = jnp.cumsum(has.astype(I32)) - 1
        later = jnp.where(jnp.logical_and(has[None, :], eid[None, :] > eid[:, None]), eid[None, :], E)
        nxt = jnp.min(later, axis=1)
        next_e = jnp.where(nxt < E, nxt, -1).astype(I32)[block_e]
        parity = (ordinal % 2).astype(I32)[block_e]

        n_used = n_used.reshape(1)
        xs = _dispatch(dst8, n8f, src8, units, tail0, tailn, n_used, pos, h2, n_rows, ts, tm)
        ys = _experts(block_e, n_used, next_e, parity, xs, w1, b1, w2, b2, l, tm)
        x = _combine(dst8, n8f, src8, units, x1, mod, l, gp, g_final, ys, E, ts, final=(l == L - 1))
    return x
```

```python
import functools

import jax
import jax.numpy as jnp
from jax import lax
from jax.experimental import pallas as pl
from jax.experimental.pallas import tpu as pltpu

F32 = jnp.float32
BF16 = jnp.bfloat16
I32 = jnp.int32

HEAD_DIM = 64
N_Q_HEADS = 8
N_KV_HEADS = 2
ATTN_WIDTH = N_Q_HEADS * HEAD_DIM
KV_WIDTH = N_KV_HEADS * HEAD_DIM
GRID_W = 64
ROPE_THETA = 10000.0
ROPE_FREQS = HEAD_DIM // 4
CONV_WIDTH = 31
CONV_PAD = CONV_WIDTH // 2
TOP_K = 4
SWIGLU_LIMIT = 7.0
SWIGLU_ALPHA = 1.702
EPS = 1e-6
LOG2E = 1.4426950408889634

LANES = 128
SUB = 8
VMEM_LIMIT = 56 * 1024 * 1024


def _cparams(*sem):
    return pltpu.CompilerParams(dimension_semantics=sem, vmem_limit_bytes=VMEM_LIMIT)


def _stage_rows(ts, n_exp):
    return ts * TOP_K + n_exp * SUB


def _mod_kernel(c_ref, w_ref, b_ref, o_ref):
    c = c_ref[...]
    c_act = (c * jax.nn.sigmoid(c)).astype(BF16)
    o_ref[0] = jnp.dot(c_act, w_ref[0].astype(BF16), preferred_element_type=F32) + b_ref[0]


def _modulation(c, w_mod, b_mod):
    L, D, W = w_mod.shape
    B = c.shape[0]
    tn = 1536
    return pl.pallas_call(
        _mod_kernel,
        grid=(L, W // tn),
        in_specs=[pl.BlockSpec((B, D), lambda l, j: (0, 0)),
                  pl.BlockSpec((1, D, tn), lambda l, j: (l, 0, j)),
                  pl.BlockSpec((1, 1, tn), lambda l, j: (l, 0, j))],
        out_specs=pl.BlockSpec((1, B, tn), lambda l, j: (l, 0, j)),
        out_shape=jax.ShapeDtypeStruct((L, B, W), F32),
        compiler_params=_cparams("parallel", "parallel"),
        name="modulation",
    )(c, w_mod, b_mod.reshape(L, 1, W))


def _head_rmsnorm(t, gain, bd):
    msq = jnp.dot((t * t).astype(BF16), bd, preferred_element_type=F32)
    return t * lax.rsqrt(msq + EPS) * gain


def _rope(t, cos, sin):
    w = t.shape[1]
    reps = w // LANES
    cosw = jnp.concatenate([cos] * reps, axis=1) if reps > 1 else cos
    sinw = jnp.concatenate([sin] * reps, axis=1) if reps > 1 else sin
    lane = lax.broadcasted_iota(I32, t.shape, 1)
    first = (lane & ROPE_FREQS) == 0
    partner = jnp.where(first, pltpu.roll(t, w - ROPE_FREQS, 1), pltpu.roll(t, ROPE_FREQS, 1))
    return t * cosw + partner * sinw


def _inproj_kernel(x_ref, mod_ref, gmix_ref, w_ref, gq_ref, gk_ref, bd_ref, cos_ref, sin_ref,
                   q_ref, k_ref, v_ref, u_ref, wb_ref):
    @pl.when(jnp.logical_and(pl.program_id(0) == 0, pl.program_id(1) == 0))
    def _():
        wb_ref[...] = w_ref[0].astype(BF16)

    x = x_ref[0]
    mod = mod_ref[0]
    sh1, sc1 = mod[0:1], mod[1:2]
    ms = jnp.mean(x * x, axis=-1, keepdims=True)
    xn = x * lax.rsqrt(ms + EPS) * gmix_ref[...]
    h = (xn * (1.0 + sc1) + sh1).astype(BF16)
    proj = jnp.dot(h, wb_ref[...], preferred_element_type=F32)
    a0, a1, a2 = ATTN_WIDTH, ATTN_WIDTH + KV_WIDTH, ATTN_WIDTH + 2 * KV_WIDTH
    cw = (proj.shape[1] - a2) // 2
    q, k, v = proj[:, :a0], proj[:, a0:a1], proj[:, a1:a2]
    ca, cg = proj[:, a2:a2 + cw], proj[:, a2 + cw:]
    cos, sin = cos_ref[...], sin_ref[...]
    bd = bd_ref[...]
    q = _rope(_head_rmsnorm(q, gq_ref[...], bd), cos, sin)
    k = _rope(_head_rmsnorm(k, gk_ref[...], bd[:KV_WIDTH, :KV_WIDTH]), cos, sin)
    q_ref[0] = (q * (HEAD_DIM ** -0.5 * LOG2E)).astype(BF16)
    k_ref[0] = k.astype(BF16)
    v_ref[0] = v.astype(BF16)
    u_ref[0] = ca * jax.nn.sigmoid(cg)


def _inproj(x, mod, layer, g_mix, w_in, g_q, g_k, bd, cos_t, sin_t, ts):
    B, S, D = x.shape
    W = w_in.shape[2]
    cw = (W - ATTN_WIDTH - 2 * KV_WIDTH) // 2
    gq = jnp.tile(g_q, N_Q_HEADS).reshape(1, ATTN_WIDTH)
    gk = jnp.tile(g_k, N_KV_HEADS).reshape(1, KV_WIDTH)
    const = lambda b, s: (0, 0)
    return pl.pallas_call(
        _inproj_kernel,
        grid=(B, S // ts),
        in_specs=[pl.BlockSpec((1, ts, D), lambda b, s: (b, s, 0)),
                  pl.BlockSpec((None, 1, 6, D), lambda b, s: (layer, b, 0, 0)),
                  pl.BlockSpec((1, D), const),
                  pl.BlockSpec((1, D, W), lambda b, s: (layer, 0, 0)),
                  pl.BlockSpec((1, ATTN_WIDTH), const),
                  pl.BlockSpec((1, KV_WIDTH), const),
                  pl.BlockSpec((ATTN_WIDTH, ATTN_WIDTH), const),
                  pl.BlockSpec((ts, LANES), lambda b, s: (s, 0)),
                  pl.BlockSpec((ts, LANES), lambda b, s: (s, 0))],
        out_specs=[pl.BlockSpec((1, ts, ATTN_WIDTH), lambda b, s: (b, s, 0)),
                   pl.BlockSpec((1, ts, KV_WIDTH), lambda b, s: (b, s, 0)),
                   pl.BlockSpec((1, ts, KV_WIDTH), lambda b, s: (b, s, 0)),
                   pl.BlockSpec((1, ts, cw), lambda b, s: (b, s, 0))],
        out_shape=[jax.ShapeDtypeStruct((B, S, ATTN_WIDTH), BF16),
                   jax.ShapeDtypeStruct((B, S, KV_WIDTH), BF16),
                   jax.ShapeDtypeStruct((B, S, KV_WIDTH), BF16),
                   jax.ShapeDtypeStruct((B, S, cw), F32)],
        scratch_shapes=[pltpu.VMEM((D, W), BF16)],
        compiler_params=_cparams("arbitrary", "arbitrary"),
        name="inproj",
    )(x, mod, g_mix.reshape(1, D), w_in, gq, gk, bd, cos_t, sin_t)


def _attn_kernel(q_ref, k_ref, v_ref, o_ref):
    k = k_ref[0]
    v = v_ref[0]
    tq = q_ref.shape[1]
    lane = lax.broadcasted_iota(I32, (tq, LANES), 1)
    low = lane < HEAD_DIM
    heads_per_tile = LANES // HEAD_DIM
    group = N_Q_HEADS // N_KV_HEADS

    def scores(h):
        j, hh = divmod(h, heads_per_tile)
        g = h // group
        qt = q_ref[0, :, j * LANES:(j + 1) * LANES].astype(F32)
        qm = jnp.where(low if hh == 0 else jnp.logical_not(low), qt, 0.0)
        if hh != g:
            qm = pltpu.roll(qm, HEAD_DIM, 1)
        return lax.dot_general(qm.astype(BF16), k, (((1,), (1,)), ((), ())),
                               preferred_element_type=F32)

    s_next = scores(0)
    placed = []
    for h in range(N_Q_HEADS):
        s = s_next
        if h + 1 < N_Q_HEADS:
            s_next = scores(h + 1)
        j, hh = divmod(h, heads_per_tile)
        m = jnp.max(s, axis=-1, keepdims=True)
        p = jnp.exp2(s - m)
        l = jnp.sum(p, axis=-1, keepdims=True)
        o = jnp.dot(p.astype(BF16), v, preferred_element_type=F32) / l
        if hh != h // group:
            o = pltpu.roll(o, HEAD_DIM, 1)
        placed.append(o)
        if hh == heads_per_tile - 1:
            o_ref[0, :, j * LANES:(j + 1) * LANES] = jnp.where(low, placed[0], placed[1]).astype(BF16)
            placed = []


def _attention(q, k, v, tq):
    B, S, _ = q.shape
    return pl.pallas_call(
        _attn_kernel,
        grid=(B, S // tq),
        in_specs=[pl.BlockSpec((1, tq, ATTN_WIDTH), lambda b, i: (b, i, 0)),
                  pl.BlockSpec((1, S, KV_WIDTH), lambda b, i: (b, 0, 0)),
                  pl.BlockSpec((1, S, KV_WIDTH), lambda b, i: (b, 0, 0))],
        out_specs=pl.BlockSpec((1, tq, ATTN_WIDTH), lambda b, i: (b, i, 0)),
        out_shape=jax.ShapeDtypeStruct((B, S, ATTN_WIDTH), BF16),
        compiler_params=_cparams("parallel", "parallel"),
        name="attention",
    )(q, k, v)


def _conv_kernel(u_ref, w_ref, b_ref, g_ref, beta_ref, o_ref, pad_ref, *, tc):
    S, C = u_ref.shape[1], u_ref.shape[2]
    halo = 2 * SUB
    pad_ref[0:halo, :] = jnp.zeros((halo, C), F32)
    pad_ref[halo + S:halo + S + halo, :] = jnp.zeros((halo, C), F32)
    pad_ref[halo:halo + S, :] = u_ref[0]
    span = tc + 2 * halo

    def tile(i, carry):
        s0 = pl.multiple_of(i * tc, tc)
        cols = []
        for c0 in range(0, C, LANES):
            win = pad_ref[pl.ds(s0, span), c0:c0 + LANES]
            acc = jnp.zeros((tc, LANES), F32)
            for r in range(SUB):
                shifted = win if r == 0 else pltpu.roll(win, span - r, 0)
                for j in range(CONV_WIDTH):
                    off = halo - CONV_PAD + j
                    if off % SUB == r:
                        a = off - r
                        acc = acc + shifted[a:a + tc, :] * w_ref[j:j + 1, c0:c0 + LANES]
            cols.append(acc)
        acc = jnp.concatenate(cols, axis=1) + b_ref[...]
        mu = jnp.mean(acc, axis=-1, keepdims=True)
        d = acc - mu
        var = jnp.mean(d * d, axis=-1, keepdims=True)
        y = d * lax.rsqrt(var + EPS) * g_ref[...] + beta_ref[...]
        o_ref[0, pl.ds(s0, tc), :] = (y * jax.nn.sigmoid(y)).astype(BF16)
        return carry

    lax.fori_loop(0, S // tc, tile, 0)


def _conv(u, w_dw, b_dw, g_cn, b_cn):
    B, S, C = u.shape
    tc = _tile_rows(S, (128, 64))
    const = lambda b: (0, 0)
    return pl.pallas_call(
        functools.partial(_conv_kernel, tc=tc),
        grid=(B,),
        in_specs=[pl.BlockSpec((1, S, C), lambda b: (b, 0, 0)),
                  pl.BlockSpec((CONV_WIDTH, C), const),
                  pl.BlockSpec((1, C), const),
                  pl.BlockSpec((1, C), const),
                  pl.BlockSpec((1, C), const)],
        out_specs=pl.BlockSpec((1, S, C), lambda b: (b, 0, 0)),
        out_shape=jax.ShapeDtypeStruct((B, S, C), BF16),
        scratch_shapes=[pltpu.VMEM((S + 4 * SUB, C), F32)],
        compiler_params=_cparams("parallel"),
        name="conv",
    )(u, w_dw, b_dw.reshape(1, C), g_cn.reshape(1, C), b_cn.reshape(1, C))


def _outproj_router_kernel(x_ref, attn_ref, conv_ref, wout_ref, mod_ref, gffn_ref, wr_ref, br_ref, tri_ref, etri_ref,
                           x1_ref, h2_ref, pos_ref, gp_ref, a8_ref, n8_ref, b8_ref, tot_ref, carry_ref, wb_ref):
    first_step = jnp.logical_and(pl.program_id(0) == 0, pl.program_id(1) == 0)

    @pl.when(first_step)
    def _():
        carry_ref[...] = jnp.zeros_like(carry_ref)
        wb_ref[...] = wout_ref[0].astype(BF16)

    mod = mod_ref[0]
    gt1, sh2, sc2 = mod[2:3], mod[3:4], mod[4:5]
    aw = attn_ref.shape[2]
    mix = (jnp.dot(attn_ref[0], wb_ref[:aw, :], preferred_element_type=F32)
           + jnp.dot(conv_ref[0], wb_ref[aw:, :], preferred_element_type=F32))
    x1 = x_ref[0] + gt1 * mix
    x1_ref[0] = x1
    ms = jnp.mean(x1 * x1, axis=-1, keepdims=True)
    h2 = x1 * lax.rsqrt(ms + EPS) * gffn_ref[...] * (1.0 + sc2) + sh2
    h2b = h2.astype(BF16)
    h2_ref[...] = h2b

    logits = lax.dot_general(wr_ref[...].astype(BF16), h2b, (((1,), (1,)), ((), ())),
                             preferred_element_type=F32) + br_ref[...]
    n_exp, ts = logits.shape
    reps = ts // LANES
    eio = lax.broadcasted_iota(I32, logits.shape, 0).astype(F32)
    vals, hits = [], []
    work = logits
    for _ in range(TOP_K):
        m = jnp.max(work, axis=0, keepdims=True)
        ik = jnp.min(jnp.where(work == m, eio, float(n_exp)), axis=0, keepdims=True)
        hit = eio == ik
        work = jnp.where(hit, -jnp.inf, work)
        vals.append(m)
        hits.append(hit)
    ex = [jnp.exp(v - vals[0]) for v in vals]
    den = ex[0] + ex[1] + ex[2] + ex[3]

    chosen = jnp.zeros(logits.shape, F32)
    for hit in hits:
        chosen = chosen + hit.astype(F32)
    prefix = jnp.dot(chosen.astype(BF16), tri_ref[...], preferred_element_type=F32)
    cnt = jnp.sum(chosen, axis=1, keepdims=True)
    n8 = jnp.broadcast_to(jnp.floor((cnt + (SUB - 1.0)) * (1.0 / SUB)), (n_exp, LANES))
    b8 = jnp.dot(etri_ref[...], n8.astype(BF16), preferred_element_type=F32)
    slot = jnp.concatenate([b8 * float(SUB)] * reps, axis=1) + prefix
    pos = [jnp.sum(jnp.where(hit, slot, 0.0), axis=0, keepdims=True) for hit in hits]
    for kk in range(TOP_K):
        pos_ref[kk:kk + 1, :] = pos[kk].astype(I32)
    a8_ref[0] = carry_ref[...].astype(I32)
    n8_ref[0] = n8.astype(I32)
    b8_ref[0] = b8.astype(I32)
    carry_ref[...] = carry_ref[...] + n8
    tot_ref[...] = carry_ref[...]

    rows = [e / den for e in ex] + pos + [jnp.zeros((LANES - 2 * TOP_K, ts), F32)]
    gp_ref[...] = jnp.concatenate(rows, axis=0).T


def _outproj_router(x, attn, conv, w_out, mod, layer, g_ffn, w_router, b_router, tri, etri, ts):
    B, S, D = x.shape
    N = B * S
    E = w_router.shape[1]
    nts = S // ts
    aw, cw = attn.shape[2], conv.shape[2]
    const = lambda b, s: (0, 0)
    tok = lambda b, s: (0, b * nts + s)
    tile3 = lambda b, s: (b * nts + s, 0, 0)
    return pl.pallas_call(
        _outproj_router_kernel,
        grid=(B, nts),
        in_specs=[pl.BlockSpec((1, ts, D), lambda b, s: (b, s, 0)),
                  pl.BlockSpec((1, ts, aw), lambda b, s: (b, s, 0)),
                  pl.BlockSpec((1, ts, cw), lambda b, s: (b, s, 0)),
                  pl.BlockSpec((1, aw + cw, D), lambda b, s: (layer, 0, 0)),
                  pl.BlockSpec((None, 1, 6, D), lambda b, s: (layer, b, 0, 0)),
                  pl.BlockSpec((1, D), const),
                  pl.BlockSpec((E, D), const),
                  pl.BlockSpec((E, 1), const),
                  pl.BlockSpec((ts, ts), const),
                  pl.BlockSpec((E, E), const)],
        out_specs=[pl.BlockSpec((1, ts, D), lambda b, s: (b, s, 0)),
                   pl.BlockSpec((ts, D), lambda b, s: (b * nts + s, 0)),
                   pl.BlockSpec((TOP_K, ts), tok),
                   pl.BlockSpec((ts, LANES), lambda b, s: (b * nts + s, 0)),
                   pl.BlockSpec((1, E, LANES), tile3),
                   pl.BlockSpec((1, E, LANES), tile3),
                   pl.BlockSpec((1, E, LANES), tile3),
                   pl.BlockSpec((E, LANES), const)],
        out_shape=[jax.ShapeDtypeStruct((B, S, D), F32),
                   jax.ShapeDtypeStruct((N, D), BF16),
                   jax.ShapeDtypeStruct((TOP_K, N), I32),
                   jax.ShapeDtypeStruct((N, LANES), F32),
                   jax.ShapeDtypeStruct((B * nts, E, LANES), I32),
                   jax.ShapeDtypeStruct((B * nts, E, LANES), I32),
                   jax.ShapeDtypeStruct((B * nts, E, LANES), I32),
                   jax.ShapeDtypeStruct((E, LANES), F32)],
        scratch_shapes=[pltpu.VMEM((E, LANES), F32), pltpu.VMEM((aw + cw, D), BF16)],
        compiler_params=_cparams("arbitrary", "arbitrary"),
        name="outproj_router",
    )(x, attn, conv, w_out, mod, g_ffn.reshape(1, D), w_router.T, b_router.reshape(E, 1), tri, etri)


def _start_segment_copies(n8_ref, a_ref, b_ref, base, n_exp, max_units, copy):
    n_bits = max_units.bit_length()
    common = min(4, n_bits)
    for e in range(n_exp):
        n = n8_ref[base + e]
        a0 = a_ref[base + e]
        b0 = b_ref[base + e]

        def chunk(bit):
            done = (n >> (bit + 1)) << (bit + 1)

            @pl.when(((n >> bit) & 1) == 1)
            def _():
                copy(pl.multiple_of((a0 + done) * SUB, SUB), pl.multiple_of((b0 + done) * SUB, SUB),
                     SUB << bit).start()

        for bit in reversed(range(common)):
            chunk(bit)

        def rare(_, c):
            for bit in reversed(range(common, n_bits)):
                chunk(bit)
            return c

        lax.fori_loop(0, (n >> common != 0).astype(I32), rare, 0)


def _wait_segment_copies(units, big_units, copy):
    shift = big_units.bit_length() - 1
    mid = min(8, big_units)
    lax.fori_loop(0, units >> shift, lambda u, c: (copy(0, 0, SUB * big_units).wait(), c)[1], 0)
    lax.fori_loop(0, (units & (big_units - 1)) >> (mid.bit_length() - 1),
                  lambda u, c: (copy(0, 0, SUB * mid).wait(), c)[1], 0)
    lax.fori_loop(0, units & (mid - 1), lambda u, c: (copy(0, 0, SUB).wait(), c)[1], 0)


def _dispatch_kernel(dst8_ref, n8_ref, src8_ref, units_ref, tail0_ref, tailn_ref, nu_ref,
                     pos_ref, h_ref, xs_ref, stage_ref, zero_ref, sem, zsem, *, chunk):
    j = pl.program_id(0)
    last = pl.num_programs(0) - 1
    slot = j % 2
    ts = h_ref.shape[0]
    n_exp = tail0_ref.shape[0]
    rows = stage_ref.shape[1]
    h = h_ref[...]
    pos = pos_ref[...]
    for r0 in range(0, rows, chunk):
        rio = lax.broadcasted_iota(I32, (chunk, ts), 0) + r0
        onehot = jnp.zeros((chunk, ts), F32)
        for kk in range(TOP_K):
            onehot = jnp.where(rio == pos[kk:kk + 1, :], 1.0, onehot)
        stage_ref[slot, r0:r0 + chunk, :] = jnp.dot(onehot.astype(BF16), h, preferred_element_type=F32)

    def seg_copy(sl):
        def build(src, dst, nrows):
            return pltpu.make_async_copy(stage_ref.at[sl, pl.ds(src, nrows)], xs_ref.at[pl.ds(dst, nrows)],
                                         sem.at[sl])
        return build

    _start_segment_copies(n8_ref, src8_ref, dst8_ref, j * n_exp, n_exp, ts // SUB, seg_copy(slot))

    @pl.when(j > 0)
    def _():
        _wait_segment_copies(units_ref[j - 1], ts // SUB, seg_copy(1 - slot))

    @pl.when(j == last)
    def _():
        zero_ref[...] = jnp.zeros_like(zero_ref)

        def zero_copy(u):
            return pltpu.make_async_copy(zero_ref.at[pl.ds(0, SUB)],
                                         xs_ref.at[pl.ds(pl.multiple_of(u * SUB, SUB), SUB)], zsem)

        for e in range(n_exp):
            lo = tail0_ref[e]
            hi = lo + tailn_ref[e]
            lax.fori_loop(lo, hi, lambda u, c: (zero_copy(u).start(), c)[1], 0)
            lax.fori_loop(lo, hi, lambda u, c: (zero_copy(u).wait(), c)[1], 0)

        tm = zero_ref.shape[0]

        def zero_block(b):
            return pltpu.make_async_copy(zero_ref, xs_ref.at[pl.ds(pl.multiple_of(b * tm, tm), tm)], zsem)

        n_blocks = xs_ref.shape[0] // tm
        lax.fori_loop(nu_ref[0], n_blocks, lambda b, c: (zero_block(b).start(), c)[1], 0)
        lax.fori_loop(nu_ref[0], n_blocks, lambda b, c: (zero_block(b).wait(), c)[1], 0)
        _wait_segment_copies(units_ref[j], ts // SUB, seg_copy(slot))


def _dispatch(dst8, n8, src8, units, tail0, tailn, n_used, pos, h2, n_rows, ts, tm):
    N, D = h2.shape
    E = tail0.shape[0]
    rows = _stage_rows(ts, E)
    chunk = 256 if rows % 256 == 0 else LANES
    return pl.pallas_call(
        functools.partial(_dispatch_kernel, chunk=chunk),
        grid_spec=pltpu.PrefetchScalarGridSpec(
            num_scalar_prefetch=7,
            grid=(N // ts,),
            in_specs=[pl.BlockSpec((TOP_K, ts), lambda i, *_: (0, i)),
                      pl.BlockSpec((ts, D), lambda i, *_: (i, 0))],
            out_specs=pl.BlockSpec(memory_space=pl.ANY),
            scratch_shapes=[pltpu.VMEM((2, rows, D), F32),
                            pltpu.VMEM((tm, D), F32),
                            pltpu.SemaphoreType.DMA((2,)),
                            pltpu.SemaphoreType.DMA]),
        out_shape=jax.ShapeDtypeStruct((n_rows, D), F32),
        compiler_params=_cparams("arbitrary"),
        name="dispatch",
    )(dst8, n8, src8, units, tail0, tailn, n_used, pos, h2)


def _expert_kernel(be_ref, nu_ref, nx_ref, par_ref, x_ref, b1_ref, b2_ref, w1_hbm, w2_hbm, y_ref,
                   w1f_ref, w2f_ref, w1b_ref, w2b_ref, sem, *, layer):
    i = pl.program_id(0)
    live = i < nu_ref[0]
    e = be_ref[i]
    fresh = jnp.logical_or(i == 0, e != be_ref[jnp.maximum(i - 1, 0)])
    slot = par_ref[i]

    def weight_copies(expert, sl):
        return (pltpu.make_async_copy(w1_hbm.at[layer, expert], w1f_ref.at[sl], sem.at[0, sl]),
                pltpu.make_async_copy(w2_hbm.at[layer, expert], w2f_ref.at[sl], sem.at[1, sl]))

    @pl.when(i == 0)
    def _():
        for cp in weight_copies(e, slot):
            cp.start()

    @pl.when(jnp.logical_and(live, fresh))
    def _():
        for cp in weight_copies(e, slot):
            cp.wait()
        w1b_ref[...] = w1f_ref[slot].astype(BF16)
        w2b_ref[...] = w2f_ref[slot].astype(BF16)
        nxt = nx_ref[i]

        @pl.when(nxt >= 0)
        def _():
            for cp in weight_copies(nxt, 1 - slot):
                cp.start()

    @pl.when(live)
    def _():
        dx = w2b_ref.shape[0]
        hu = jnp.dot(x_ref[...].astype(BF16), w1b_ref[...], preferred_element_type=F32) + b1_ref[0, 0]
        x_glu = jnp.minimum(hu[:, :dx], SWIGLU_LIMIT)
        x_lin = jnp.clip(hu[:, dx:], -SWIGLU_LIMIT, SWIGLU_LIMIT)
        act = (x_lin + 1.0) * (x_glu * jax.nn.sigmoid(SWIGLU_ALPHA * x_glu))
        y_ref[...] = jnp.dot(act.astype(BF16), w2b_ref[...], preferred_element_type=F32) + b2_ref[0, 0]

    @pl.when(jnp.logical_not(live))
    def _():
        y_ref[...] = jnp.zeros_like(y_ref)


def _experts(block_e, n_used, next_e, parity, xs, w1, b1, w2, b2, layer, tm):
    P, D = xs.shape
    _, E, _, H = w1.shape
    dx = w2.shape[2]
    row = lambda i, be, nu, nx, par: (jnp.minimum(i, nu[0] - 1), 0)
    exp4 = lambda i, be, nu, nx, par: (layer, be[i], 0, 0)
    return pl.pallas_call(
        functools.partial(_expert_kernel, layer=layer),
        grid_spec=pltpu.PrefetchScalarGridSpec(
            num_scalar_prefetch=4,
            grid=(P // tm,),
            in_specs=[pl.BlockSpec((tm, D), row),
                      pl.BlockSpec((1, 1, 1, H), exp4),
                      pl.BlockSpec((1, 1, 1, D), exp4),
                      pl.BlockSpec(memory_space=pl.ANY),
                      pl.BlockSpec(memory_space=pl.ANY)],
            out_specs=pl.BlockSpec((tm, D), lambda i, *_: (i, 0)),
            scratch_shapes=[pltpu.VMEM((2, D, H), F32), pltpu.VMEM((2, dx, D), F32),
                            pltpu.VMEM((D, H), BF16), pltpu.VMEM((dx, D), BF16),
                            pltpu.SemaphoreType.DMA((2, 2))]),
        out_shape=jax.ShapeDtypeStruct((P, D), F32),
        compiler_params=_cparams("arbitrary"),
        name="experts",
    )(block_e, n_used, next_e, parity, xs,
      b1.reshape(b1.shape[0], E, 1, H), b2.reshape(b2.shape[0], E, 1, D), w1, w2)


def _combine_kernel(dst8_ref, n8_ref, src8_ref, units_ref,
                    x_ref, mod_ref, gp_ref, gfin_ref, ys_ref, o_ref, stage_ref, sem, *, final, chunk):
    j = pl.program_id(0) * pl.num_programs(1) + pl.program_id(1)
    n_tiles = pl.num_programs(0) * pl.num_programs(1)
    slot = j % 2
    ts = x_ref.shape[1]
    rows = stage_ref.shape[1]
    n_exp = (rows - ts * TOP_K) // SUB

    def seg_copy(sl):
        def build(src, dst, nrows):
            return pltpu.make_async_copy(ys_ref.at[pl.ds(dst, nrows)], stage_ref.at[sl, pl.ds(src, nrows)],
                                         sem.at[sl])
        return build

    def start_fetch(tile, sl):
        stage_ref[sl, ts * TOP_K:rows, :] = jnp.zeros((rows - ts * TOP_K, stage_ref.shape[2]), F32)
        _start_segment_copies(n8_ref, src8_ref, dst8_ref, tile * n_exp, n_exp, ts // SUB, seg_copy(sl))

    @pl.when(j == 0)
    def _():
        start_fetch(0, 0)

    @pl.when(j + 1 < n_tiles)
    def _():
        start_fetch(j + 1, 1 - slot)

    gp = gp_ref[...]
    gates = [jnp.broadcast_to(gp[:, kk:kk + 1], (ts, chunk)) for kk in range(TOP_K)]
    slots = [jnp.broadcast_to(gp[:, TOP_K + kk:TOP_K + kk + 1].astype(I32), (ts, chunk)) for kk in range(TOP_K)]
    _wait_segment_copies(units_ref[j], ts // SUB, seg_copy(slot))

    moe = jnp.zeros((ts, stage_ref.shape[2]), F32)
    for r0 in range(0, rows, chunk):
        rio = lax.broadcasted_iota(I32, (ts, chunk), 1) + r0
        wgt = jnp.zeros((ts, chunk), F32)
        for kk in range(TOP_K):
            wgt = jnp.where(rio == slots[kk], gates[kk], wgt)
        moe = moe + jnp.dot(wgt.astype(BF16), stage_ref[slot, r0:r0 + chunk, :].astype(BF16),
                            preferred_element_type=F32)
    gt2 = mod_ref[0][5:6]
    out = x_ref[0] + gt2 * moe
    if final:
        ms = jnp.mean(out * out, axis=-1, keepdims=True)
        out = out * lax.rsqrt(ms + EPS) * gfin_ref[...]
    o_ref[0] = out


def _combine(dst8, n8, src8, units, x1, mod, layer, gp, g_final, ys, n_exp, ts, final):
    B, S, D = x1.shape
    nts = S // ts
    rows = _stage_rows(ts, n_exp)
    chunk = 256 if rows % 256 == 0 else LANES
    return pl.pallas_call(
        functools.partial(_combine_kernel, final=final, chunk=chunk),
        grid_spec=pltpu.PrefetchScalarGridSpec(
            num_scalar_prefetch=4,
            grid=(B, nts),
            in_specs=[pl.BlockSpec((1, ts, D), lambda b, s, *_: (b, s, 0)),
                      pl.BlockSpec((None, 1, 6, D), lambda b, s, *_: (layer, b, 0, 0)),
                      pl.BlockSpec((ts, LANES), lambda b, s, *_: (b * nts + s, 0)),
                      pl.BlockSpec((1, D), lambda b, s, *_: (0, 0)),
                      pl.BlockSpec(memory_space=pl.ANY)],
            out_specs=pl.BlockSpec((1, ts, D), lambda b, s, *_: (b, s, 0)),
            scratch_shapes=[pltpu.VMEM((2, rows, D), F32), pltpu.SemaphoreType.DMA((2,))]),
        out_shape=jax.ShapeDtypeStruct((B, S, D), F32),
        compiler_params=_cparams("arbitrary", "arbitrary"),
        name="combine",
    )(dst8, n8, src8, units, x1, mod, gp, g_final.reshape(1, D), ys)


def _rope_tables(seq_len):
    freqs = ROPE_THETA ** (-jnp.arange(ROPE_FREQS, dtype=F32) / ROPE_FREQS)
    pos = jnp.arange(seq_len, dtype=I32)
    row_ang = (pos // GRID_W).astype(F32)[:, None] * freqs
    col_ang = (pos % GRID_W).astype(F32)[:, None] * freqs
    cos = jnp.concatenate([jnp.cos(row_ang)] * 2 + [jnp.cos(col_ang)] * 2, axis=1)
    sin = jnp.concatenate([-jnp.sin(row_ang), jnp.sin(row_ang), -jnp.sin(col_ang), jnp.sin(col_ang)], axis=1)
    reps = LANES // HEAD_DIM
    return jnp.tile(cos, (1, reps)), jnp.tile(sin, (1, reps))


def _tile_rows(n, prefs):
    for t in prefs:
        if n % t == 0:
            return t
    return n


def kernel(x, c, w_mod, b_mod, g_mix, w_in, g_q, g_k, w_dw, b_dw, g_cn, b_cn,
           w_out, g_ffn, w_router, b_router, w1, b1, w2, b2, g_final):
    B, S, D = x.shape
    L = w_mod.shape[0]
    E = w_router.shape[2]
    N = B * S
    ts = _tile_rows(S, (512, 256, 128))
    tq = _tile_rows(S, (256, 128))
    tm = 512
    n_tiles = N // ts
    max_rows = N * TOP_K + n_tiles * E * (SUB - 1)
    n_rows = (max_rows + tm - 1) // tm * tm + E * tm
    n_blocks = n_rows // tm

    mod = _modulation(c, w_mod, b_mod).reshape(L, B, 6, D)
    cos_t, sin_t = _rope_tables(S)
    head = jnp.arange(ATTN_WIDTH, dtype=I32) // HEAD_DIM
    bd = jnp.where(head[:, None] == head[None, :], 1.0 / HEAD_DIM, 0.0).astype(BF16)
    tok = jnp.arange(ts, dtype=I32)
    tri = (tok[:, None] < tok[None, :]).astype(BF16)
    eid = jnp.arange(E, dtype=I32)
    etri = (eid[None, :] < eid[:, None]).astype(BF16)

    for l in range(L):
        q, k, v, u = _inproj(x, mod, l, g_mix[l], w_in, g_q[l], g_k[l], bd, cos_t, sin_t, ts)
        attn = _attention(q, k, v, tq)
        conv = _conv(u, w_dw[l], b_dw[l], g_cn[l], b_cn[l])
        x1, h2, pos, gp, a8, n8, b8, tot = _outproj_router(
            x, attn, conv, w_out, mod, l, g_ffn[l], w_router[l], b_router[l], tri, etri, ts)

        rows_e = tot[:, 0].astype(I32) * SUB
        padded = (rows_e + tm - 1) // tm * tm
        pend = jnp.cumsum(padded)
        pstart = pend - padded
        n_used = (pend[-1] // tm).astype(I32)
        blk = jnp.minimum(jnp.arange(n_blocks, dtype=I32), n_used - 1) * tm
        block_e = jnp.minimum(jnp.sum(blk[:, None] >= pend[None, :], axis=1), E - 1).astype(I32)
        dst8 = (a8[:, :, 0] + (pstart // SUB)[None, :]).reshape(-1)
        n8f = n8[:, :, 0].reshape(-1)
        src8 = b8[:, :, 0].reshape(-1)
        units = jnp.sum(n8[:, :, 0], axis=1)
        tail0 = (pstart + rows_e) // SUB
        tailn = (padded - rows_e) // SUB

        has = padded > 0
        ordinal = jnp.cumsum(has.astype(I32)) - 1
        later = jnp.where(jnp.logical_and(has[None, :], eid[None, :] > eid[:, None]), eid[None, :], E)
        nxt = jnp.min(later, axis=1)
        next_e = jnp.where(nxt < E, nxt, -1).astype(I32)[block_e]
        parity = (ordinal % 2).astype(I32)[block_e]

        n_used = n_used.reshape(1)
        xs = _dispatch(dst8, n8f, src8, units, tail0, tailn, n_used, pos, h2, n_rows, ts, tm)
        ys = _experts(block_e, n_used, next_e, parity, xs, w1, b1, w2, b2, l, tm)
        x = _combine(dst8, n8f, src8, units, x1, mod, l, gp, g_final, ys, E, ts, final=(l == L - 1))
    return x
```
